```python
import jax, jax.numpy as jnp
from jax import lax
import numpy as np

D_MODEL = 2048
BATCH = 4
SEQ = 4096
DEPTH = 4

GRID_W = 64
CTX_LEN = 256
N_MIXERS = 2
N_ATTN_LAYERS = (DEPTH + N_MIXERS - 1) // N_MIXERS
N_RWKV_LAYERS = DEPTH // N_MIXERS
N_SUBLAYERS = 3
NORM_EPS = 1e-6
ATTN_HEAD_DIM = 128
ATTN_HEADS = D_MODEL // ATTN_HEAD_DIM
ATTN_KV_HEADS = ATTN_HEADS // 4
ATTN_GROUP = ATTN_HEADS // ATTN_KV_HEADS
Q_BLOCK = 128
ROPE_THETA = 10000.0
ROPE_AXIS_DIM = ATTN_HEAD_DIM // 2
RWKV_HEAD_DIM = 64
RWKV_HEADS = D_MODEL // RWKV_HEAD_DIM
DECAY_LORA = 96
ICL_LORA = 96
GATE_LORA = 256
LNX_EPS = 64e-5
D_FF = 5632

kernel_name = 'hybrid_gqa_rwkv7_macaron_dit'


def _rms(x, g):
    xf = x.astype(jnp.float32)
    y = xf * lax.rsqrt(jnp.mean(xf * xf, axis=-1, keepdims=True) + NORM_EPS)
    return y.astype(x.dtype) * g


def _modulate(x, g, shift, scale):
    return _rms(x, g) * (1 + scale) + shift


def _residual(x, out, g, gate, weight):
    return x + weight * gate * _rms(out, g)


def _swiglu(y, w_in, w_out):
    h = y @ w_in
    gate, up = h[..., :D_FF], h[..., D_FF:]
    return (jax.nn.silu(gate) * up) @ w_out


def _ffn_half(x, g_pre, g_post, mod, w_in, w_out):
    shift, scale, gate = mod
    return _residual(x, _swiglu(_modulate(x, g_pre, shift, scale), w_in, w_out), g_post, gate, 0.5)


def _axial_rope_tables(n_tokens):
    rows = n_tokens // GRID_W
    row = jnp.repeat(jnp.arange(rows, dtype=jnp.float32), GRID_W)
    col = jnp.tile(jnp.arange(GRID_W, dtype=jnp.float32), rows)
    n_freq = ROPE_AXIS_DIM // 2
    inv = ROPE_THETA ** (-jnp.arange(n_freq, dtype=jnp.float32) / n_freq)
    ang = jnp.concatenate([row[:, None] * inv, col[:, None] * inv], axis=-1)
    return jnp.cos(ang), jnp.sin(ang)


def _rope(x, cos, sin):
    shp = (cos.shape[0],) + (1,) * (x.ndim - 3) + (cos.shape[1],)
    cos = cos.reshape(shp)
    sin = sin.reshape(shp)
    xf = x.astype(jnp.float32).reshape(x.shape[:-1] + (-1, 2))
    x1, x2 = xf[..., 0], xf[..., 1]
    out = jnp.stack([x1 * cos - x2 * sin, x1 * sin + x2 * cos], axis=-1)
    return out.reshape(x.shape).astype(x.dtype)


def _attend(q, k, v):
    s = jnp.einsum('bqkgd,btkd->bkgqt', q, k).astype(jnp.float32) * (ATTN_HEAD_DIM ** -0.5)
    p = jax.nn.softmax(s, axis=-1).astype(v.dtype)
    return jnp.einsum('bkgqt,btkd->bqkgd', p, v)


def _attention_mixer(y_lat, y_ctx, w_qkv, w_o, q_gain, k_gain, cos, sin, ctx_out):
    B, S, _ = y_lat.shape
    L = y_ctx.shape[1]
    q_cols = ATTN_HEADS * ATTN_HEAD_DIM
    qkv_l = y_lat @ w_qkv
    q_l = _rope(_rms(qkv_l[..., :q_cols].reshape(B, S, ATTN_KV_HEADS, ATTN_GROUP, ATTN_HEAD_DIM), q_gain), cos, sin)
    kv_l = qkv_l[..., q_cols:].reshape(B, S, 2, ATTN_KV_HEADS, ATTN_HEAD_DIM)
    k_l = _rope(_rms(kv_l[:, :, 0], k_gain), cos, sin)
    v_l = kv_l[:, :, 1]
    kv_c = (y_ctx @ w_qkv[:, q_cols:]).reshape(B, L, 2, ATTN_KV_HEADS, ATTN_HEAD_DIM)
    k_c = _rms(kv_c[:, :, 0], k_gain)
    v_c = kv_c[:, :, 1]
    k_all = jnp.concatenate([k_c, k_l], axis=1)
    v_all = jnp.concatenate([v_c, v_l], axis=1)
    n_blk = S // Q_BLOCK
    q_blocks = jnp.moveaxis(q_l.reshape(B, n_blk, Q_BLOCK, ATTN_KV_HEADS, ATTN_GROUP, ATTN_HEAD_DIM), 1, 0)
    o_l = lax.map(lambda qb: _attend(qb, k_all, v_all), q_blocks)
    o_l = jnp.moveaxis(o_l, 0, 1).reshape(B, S, D_MODEL) @ w_o
    o_c = None
    if ctx_out:
        q_c = _rms((y_ctx @ w_qkv[:, :q_cols]).reshape(B, L, ATTN_KV_HEADS, ATTN_GROUP, ATTN_HEAD_DIM), q_gain)
        o_c = _attend(q_c, k_c, v_c).reshape(B, L, D_MODEL) @ w_o
    return o_l, o_c


def _centred_shift(y):
    pad = jnp.pad(y, ((0, 0), (1, 1), (0, 0)))
    return 0.5 * (pad[:, :-2] + pad[:, 2:])


def _heads(x):
    return x.reshape(x.shape[:2] + (RWKV_HEADS, RWKV_HEAD_DIM))


def _l2n(x):
    xf = x.astype(jnp.float32)
    return xf * lax.rsqrt(jnp.sum(xf * xf, axis=-1, keepdims=True) + 1e-12)


def _rwkv_prep(y, mu, w_rkv, g1, g2, dec_w0, dec_w1, dec_w2, icl_a0, icl_a1, icl_a2, k_k, k_a, with_out):
    xx = _centred_shift(y) - y
    mix = lambda j: y + xx * mu[j]
    xw, xk, xv, xa = mix(1), mix(2), mix(3), mix(4)
    k = xk @ w_rkv[1]
    v = _heads(xv @ w_rkv[2]).astype(jnp.float32)
    kk = _l2n(_heads(k * k_k))
    dirs = []
    for d in range(2):
        w_pre = (dec_w0[d] + jnp.tanh(xw @ dec_w1[d]) @ dec_w2[d]).astype(jnp.float32)
        decay = jnp.exp(-jnp.exp(-jax.nn.softplus(-w_pre) - 0.5))
        a = jax.nn.sigmoid(icl_a0[d] + (xa @ icl_a1[d]) @ icl_a2[d])
        k_d = _heads(k * (1 + (a - 1) * k_a)).astype(jnp.float32)
        dirs.append((_heads(decay), k_d, kk * _heads(a).astype(jnp.float32)))
    r = g = None
    if with_out:
        r = _heads(mix(0) @ w_rkv[0]).astype(jnp.float32)
        g = jax.nn.sigmoid(mix(5) @ g1) @ g2
    return r, g, v, kk, dirs


def _wkv_scan(S0, w, k, v, kk, b, r, reverse):
    xs = tuple(jnp.moveaxis(t, 1, 0) for t in (w, k, v, kk, b))
    if r is not None:
        xs = xs + (jnp.moveaxis(r, 1, 0),)

    def step(S, inp):
        w_t, k_t, v_t, kk_t, b_t = inp[:5]
        s_kk = jnp.einsum('bhvk,bhk->bhv', S, kk_t)
        S = S * w_t[:, :, None, :] - s_kk[..., None] * b_t[:, :, None, :] + v_t[..., None] * k_t[:, :, None, :]
        y = jnp.einsum('bhvk,bhk->bhv', S, inp[5]) if r is not None else None
        return S, y

    S, ys = lax.scan(step, S0, xs, reverse=reverse)
    return S, (None if r is None else jnp.moveaxis(ys, 0, 1))


def _rwkv_out(o, r, v, k_sum, g, r_k, lnx_w, lnx_b, w_o, dtype):
    B, T = o.shape[:2]
    mean = jnp.mean(o, axis=-1, keepdims=True)
    var = jnp.mean(jnp.square(o - mean), axis=-1, keepdims=True)
    o = (o - mean) * lax.rsqrt(var + LNX_EPS)
    o = o * lnx_w.reshape(RWKV_HEADS, RWKV_HEAD_DIM) + lnx_b.reshape(RWKV_HEADS, RWKV_HEAD_DIM)
    o = o + jnp.sum(r * k_sum * r_k, axis=-1, keepdims=True) * v
    return (o.reshape(B, T, D_MODEL).astype(dtype) * g) @ w_o


def _rwkv_mixer(y_lat, y_ctx, mu, w_rkv, w_o, g1, g2, dec_w0, dec_w1, dec_w2, icl_a0, icl_a1, icl_a2,
                k_k, k_a, r_k, lnx_w, lnx_b, ctx_out):
    args = (mu, w_rkv, g1, g2, dec_w0, dec_w1, dec_w2, icl_a0, icl_a1, icl_a2, k_k, k_a)
    r_l, g_l, v_l, kk_l, dirs_l = _rwkv_prep(y_lat, *args, True)
    r_c, g_c, v_c, kk_c, dirs_c = _rwkv_prep(y_ctx, *args, ctx_out)
    B = y_lat.shape[0]
    S0 = jnp.zeros((B, RWKV_HEADS, RWKV_HEAD_DIM, RWKV_HEAD_DIM), jnp.float32)
    ys_l, ys_c = [], []
    for d, rev in enumerate((False, True)):
        w_c, k_c, b_c = dirs_c[d]
        S_c, y_c = _wkv_scan(S0, w_c, k_c, v_c, kk_c, b_c, r_c, rev)
        w_l, k_l, b_l = dirs_l[d]
        _, y_l = _wkv_scan(S_c, w_l, k_l, v_l, kk_l, b_l, r_l, rev)
        ys_l.append(y_l)
        ys_c.append(y_c)
    o_l = _rwkv_out(ys_l[0] + ys_l[1], r_l, v_l, dirs_l[0][1] + dirs_l[1][1], g_l,
                    r_k, lnx_w, lnx_b, w_o, y_lat.dtype)
    o_c = None
    if ctx_out:
        o_c = _rwkv_out(ys_c[0] + ys_c[1], r_c, v_c, dirs_c[0][1] + dirs_c[1][1], g_c,
                        r_k, lnx_w, lnx_b, w_o, y_ctx.dtype)
    return o_l, o_c


def setup_inputs(seed: int = 0) -> dict:
    key = jax.random.key(seed)
    ks = jax.random.split(key, 32)
    f32 = jnp.float32
    D, F, NA, NR = D_MODEL, D_FF, N_ATTN_LAYERS, N_RWKV_LAYERS
    nrm = lambda k, shape, s: jax.random.normal(k, shape, f32) * s
    qkv_cols = (ATTN_HEADS + 2 * ATTN_KV_HEADS) * ATTN_HEAD_DIM
    return {
        'x': nrm(ks[0], (BATCH, SEQ, D), 1.0),
        'c': nrm(ks[1], (BATCH, D), 1.0),
        'ctx': nrm(ks[2], (BATCH, CTX_LEN, D), 1.0),
        'c_ctx': nrm(ks[3], (D,), 1.0),
        'ada_w': nrm(ks[4], (DEPTH, D, N_SUBLAYERS * 3 * D), 0.5 * D ** -0.5),
        'ada_b': nrm(ks[5], (DEPTH, N_SUBLAYERS * 3 * D), 0.02),
        'norm_pre': 1.0 + nrm(ks[6], (DEPTH, N_SUBLAYERS, D), 0.02),
        'norm_post': 1.0 + nrm(ks[7], (DEPTH, N_SUBLAYERS, D), 0.02),
        'ffn_w_in': nrm(ks[8], (DEPTH, 2, D, 2 * F), D ** -0.5),
        'ffn_w_out': nrm(ks[9], (DEPTH, 2, F, D), F ** -0.5),
        'attn_w_qkv': nrm(ks[10], (NA, D, qkv_cols), D ** -0.5),
        'attn_w_o': nrm(ks[11], (NA, D, D), D ** -0.5),
        'attn_q_gain': 1.0 + nrm(ks[12], (NA, ATTN_HEAD_DIM), 0.02),
        'attn_k_gain': 1.0 + nrm(ks[13], (NA, ATTN_HEAD_DIM), 0.02),
        'rwkv_mu': jax.random.uniform(ks[14], (NR, 6, D), f32),
        'rwkv_w_rkv': nrm(ks[15], (NR, 3, D, D), D ** -0.5),
        'rwkv_w_o': nrm(ks[16], (NR, D, D), D ** -0.5),
        'rwkv_g1': nrm(ks[17], (NR, D, GATE_LORA), D ** -0.5),
        'rwkv_g2': nrm(ks[18], (NR, GATE_LORA, D), GATE_LORA ** -0.5),
        'rwkv_dec_w0': jax.random.uniform(ks[19], (NR, 2, D), f32, -6.0, -1.0),
        'rwkv_dec_w1': nrm(ks[20], (NR, 2, D, DECAY_LORA), D ** -0.5),
        'rwkv_dec_w2': nrm(ks[21], (NR, 2, DECAY_LORA, D), 0.1 * DECAY_LORA ** -0.5),
        'rwkv_icl_a0': nrm(ks[22], (NR, 2, D), 0.1),
        'rwkv_icl_a1': nrm(ks[23], (NR, 2, D, ICL_LORA), D ** -0.5),
        'rwkv_icl_a2': nrm(ks[24], (NR, 2, ICL_LORA, D), 0.5 * ICL_LORA ** -0.5),
        'rwkv_k_k': 0.85 + nrm(ks[25], (NR, D), 0.02),
        'rwkv_k_a': 1.0 + nrm(ks[26], (NR, D), 0.02),
        'rwkv_r_k': nrm(ks[27], (NR, RWKV_HEADS, RWKV_HEAD_DIM), 0.1),
        'rwkv_lnx_w': 1.0 + nrm(ks[28], (NR, D), 0.02),
        'rwkv_lnx_b': nrm(ks[29], (NR, D), 0.01),
    }


def reference(x, c, ctx, c_ctx, ada_w, ada_b, norm_pre, norm_post, ffn_w_in, ffn_w_out,
              attn_w_qkv, attn_w_o, attn_q_gain, attn_k_gain,
              rwkv_mu, rwkv_w_rkv, rwkv_w_o, rwkv_g1, rwkv_g2, rwkv_dec_w0, rwkv_dec_w1, rwkv_dec_w2,
              rwkv_icl_a0, rwkv_icl_a1, rwkv_icl_a2, rwkv_k_k, rwkv_k_a, rwkv_r_k, rwkv_lnx_w, rwkv_lnx_b):
    n_tok = x.shape[1]
    cos, sin = _axial_rope_tables(n_tok)
    h = ctx
    s_lat = jax.nn.silu(c)
    s_ctx = jax.nn.silu(c_ctx)
    for i in range(DEPTH):
        last = i == DEPTH - 1
        j = i // N_MIXERS
        mod_l = (s_lat @ ada_w[i] + ada_b[i]).reshape(-1, N_SUBLAYERS, 3, D_MODEL)
        mod_l = jnp.transpose(mod_l, (1, 2, 0, 3))[:, :, :, None, :]
        mod_c = (s_ctx @ ada_w[i] + ada_b[i]).reshape(N_SUBLAYERS, 3, D_MODEL)
        x = _ffn_half(x, norm_pre[i, 0], norm_post[i, 0], mod_l[0], ffn_w_in[i, 0], ffn_w_out[i, 0])
        h = _ffn_half(h, norm_pre[i, 0], norm_post[i, 0], mod_c[0], ffn_w_in[i, 0], ffn_w_out[i, 0])
        y_l = _modulate(x, norm_pre[i, 1], mod_l[1, 0], mod_l[1, 1])
        y_c = _modulate(h, norm_pre[i, 1], mod_c[1, 0], mod_c[1, 1])
        if i % N_MIXERS == 0:
            o_l, o_c = _attention_mixer(y_l, y_c, attn_w_qkv[j], attn_w_o[j], attn_q_gain[j], attn_k_gain[j],
                                        cos, sin, not last)
        else:
            o_l, o_c = _rwkv_mixer(y_l, y_c, rwkv_mu[j], rwkv_w_rkv[j], rwkv_w_o[j], rwkv_g1[j], rwkv_g2[j],
                                   rwkv_dec_w0[j], rwkv_dec_w1[j], rwkv_dec_w2[j],
                                   rwkv_icl_a0[j], rwkv_icl_a1[j], rwkv_icl_a2[j],
                                   rwkv_k_k[j], rwkv_k_a[j], rwkv_r_k[j], rwkv_lnx_w[j], rwkv_lnx_b[j], not last)
        x = _residual(x, o_l, norm_post[i, 1], mod_l[1, 2], 1.0)
        x = _ffn_half(x, norm_pre[i, 2], norm_post[i, 2], mod_l[2], ffn_w_in[i, 1], ffn_w_out[i, 1])
        if not last:
            h = _residual(h, o_c, norm_post[i, 1], mod_c[1, 2], 1.0)
            h = _ffn_half(h, norm_pre[i, 2], norm_post[i, 2], mod_c[2], ffn_w_in[i, 1], ffn_w_out[i, 1])
    return x
```

```python
import functools
import math

import jax
import jax.numpy as jnp
from jax import lax
from jax.experimental import pallas as pl
from jax.experimental.pallas import tpu as pltpu

F32 = jnp.float32
BF16 = jnp.bfloat16

NORM_EPS = 1e-6
LNX_EPS = 64e-5
L2N_EPS = 1e-12
ATTN_HEAD_DIM = 128
ATTN_GROUP = 4
ROPE_THETA = 10000.0
GRID_W = 64
RWKV_HEAD_DIM = 64
LANES = 128
SCAN_CHUNK = 64
SCAN_LANES = 256
VMEM_LIMIT = 56 * 1024 * 1024
DECAY_SCALE = math.exp(-0.5)


def _cparams(sem):
    return pltpu.CompilerParams(dimension_semantics=sem, vmem_limit_bytes=VMEM_LIMIT)


def _largest_divisor(n, candidates):
    for c in candidates:
        if n % c == 0:
            return c
    raise ValueError(f"no tile in {candidates} divides {n}")


def _dot(a, b):
    return jnp.dot(a, b, preferred_element_type=F32)


def _dot_nt(a, b):
    return lax.dot_general(a, b, (((1,), (1,)), ((), ())), preferred_element_type=F32)


def _split2(a):
    hi = a.astype(BF16)
    lo = (a - hi.astype(F32)).astype(BF16)
    return hi, lo


def _mm3(a, b, nt=False):
    d = _dot_nt if nt else _dot
    return d(a[0], b[0]) + (d(a[0], b[1]) + d(a[1], b[0]))


def _rms_rows(x, eps):
    return x * lax.rsqrt(jnp.mean(x * x, axis=-1, keepdims=True) + eps)


def _modulate(x, g_pre, shift, scale):
    return _rms_rows(x, NORM_EPS) * g_pre * (1.0 + scale) + shift


def _ada_kernel(s_ref, w_ref, b_ref, o_ref):
    s = s_ref[...]
    s = s * jax.nn.sigmoid(s)
    o_ref[...] = _mm3(_split2(s), _split2(w_ref[...])) + b_ref[...]


def _ada_mods(s_all, ada_w, ada_b):
    depth, d, n = ada_w.shape
    tn = _largest_divisor(n, (1024, 512, 256, 128))
    rows = s_all.shape[0]
    return pl.pallas_call(
        _ada_kernel,
        grid=(depth, n // tn),
        in_specs=[
            pl.BlockSpec((rows, d), lambda i, j: (0, 0)),
            pl.BlockSpec((None, d, tn), lambda i, j: (i, 0, j)),
            pl.BlockSpec((None, 1, tn), lambda i, j: (i, 0, j)),
        ],
        out_specs=pl.BlockSpec((None, rows, tn), lambda i, j: (i, 0, j)),
        out_shape=jax.ShapeDtypeStruct((depth, rows, n), F32),
        compiler_params=_cparams(("arbitrary", "arbitrary")),
        name="ada_mods",
    )(s_all, ada_w, ada_b.reshape(depth, 1, n))


def _ffn_kernel(x_ref, mod_ref, gpre_ref, gpost_ref, wg_ref, wu_ref, wo_ref, o_ref, y_scr, acc_scr):
    f = pl.program_id(1)

    @pl.when(f == 0)
    def _():
        y = _modulate(x_ref[...], gpre_ref[...], mod_ref[0:1, :], mod_ref[1:2, :])
        y_scr[...] = y.astype(BF16)
        acc_scr[...] = jnp.zeros_like(acc_scr)

    y = y_scr[...]
    g = _dot(y, wg_ref[...])
    u = _dot(y, wu_ref[...])
    a = (g * jax.nn.sigmoid(g)) * u
    acc_scr[...] += _dot(a.astype(BF16), wo_ref[...])

    @pl.when(f == pl.num_programs(1) - 1)
    def _():
        h = _rms_rows(acc_scr[...], NORM_EPS) * gpost_ref[...]
        o_ref[...] = x_ref[...] + 0.5 * (mod_ref[2:3, :] * h)


def _ffn_half(stream, mod, g_pre, g_post, w_in, w_out, geo, rows_out):
    nt, d = stream.shape
    f_dim = w_out.shape[0]
    tm = geo["tm"]
    tf = _largest_divisor(f_dim, (512, 256, 128))
    nf = f_dim // tf
    grp = geo["grp"]
    return pl.pallas_call(
        _ffn_kernel,
        grid=(rows_out // tm, nf),
        in_specs=[
            pl.BlockSpec((tm, d), lambda i, f: (i, 0)),
            pl.BlockSpec((None, 3, d), lambda i, f: (grp(i), 0, 0)),
            pl.BlockSpec((1, d), lambda i, f: (0, 0)),
            pl.BlockSpec((1, d), lambda i, f: (0, 0)),
            pl.BlockSpec((d, tf), lambda i, f: (0, f)),
            pl.BlockSpec((d, tf), lambda i, f: (0, nf + f)),
            pl.BlockSpec((tf, d), lambda i, f: (f, 0)),
        ],
        out_specs=pl.BlockSpec((tm, d), lambda i, f: (i, 0)),
        out_shape=jax.ShapeDtypeStruct((rows_out, d), F32),
        scratch_shapes=[pltpu.VMEM((tm, d), BF16), pltpu.VMEM((tm, d), F32)],
        compiler_params=_cparams(("parallel", "arbitrary")),
        name="ffn_half",
    )(stream, mod, g_pre.reshape(1, d), g_post.reshape(1, d), w_in, w_in, w_out)


def _oproj_kernel(o_ref, x_ref, mod_ref, gpost_ref, w_ref, out_ref):
    h = _dot(o_ref[...], w_ref[...])
    h = _rms_rows(h, NORM_EPS) * gpost_ref[...]
    out_ref[...] = x_ref[...] + mod_ref[2:3, :] * h


def _oproj_residual(o, stream, mod, g_post, w_o, geo, rows_out):
    d = stream.shape[1]
    tm = geo["tm"]
    grp = geo["grp"]
    return pl.pallas_call(
        _oproj_kernel,
        grid=(rows_out // tm,),
        in_specs=[
            pl.BlockSpec((tm, d), lambda i: (i, 0)),
            pl.BlockSpec((tm, d), lambda i: (i, 0)),
            pl.BlockSpec((None, 3, d), lambda i: (grp(i), 0, 0)),
            pl.BlockSpec((1, d), lambda i: (0, 0)),
            pl.BlockSpec((d, d), lambda i: (0, 0)),
        ],
        out_specs=pl.BlockSpec((tm, d), lambda i: (i, 0)),
        out_shape=jax.ShapeDtypeStruct((rows_out, d), F32),
        compiler_params=_cparams(("parallel",)),
        name="oproj_residual",
    )(o, stream, mod, g_post.reshape(1, d), w_o)


def _qkv_kernel(x_ref, mod_ref, gpre_ref, w_ref, gains_ref, cos_ref, sin_ref, q_ref, kt_ref, v_ref,
                *, n_q_heads, n_kv_heads, q_scale):
    hd = ATTN_HEAD_DIM
    y = _modulate(x_ref[...], gpre_ref[...], mod_ref[0:1, :], mod_ref[1:2, :]).astype(BF16)
    cos = cos_ref[...]
    sin = sin_ref[...]

    def head_post(h, gain):
        hn = _rms_rows(h, NORM_EPS) * gain
        return hn * cos + pltpu.roll(hn, hd // 2, 1) * sin

    for c in range(n_q_heads // ATTN_GROUP):
        w = ATTN_GROUP * hd
        acc = _dot(y, w_ref[:, c * w:(c + 1) * w])
        for g in range(ATTN_GROUP):
            h = head_post(acc[:, g * hd:(g + 1) * hd], gains_ref[0:1, :]) * q_scale
            q_ref[:, (c * ATTN_GROUP + g) * hd:(c * ATTN_GROUP + g + 1) * hd] = h.astype(BF16)
    q_cols = n_q_heads * hd
    kv_cols = n_kv_heads * hd
    acc = _dot(y, w_ref[:, q_cols:q_cols + kv_cols])
    for h_i in range(n_kv_heads):
        kh = head_post(acc[:, h_i * hd:(h_i + 1) * hd], gains_ref[1:2, :])
        kt_ref[h_i * hd:(h_i + 1) * hd, :] = kh.T.astype(BF16)
    v_ref[...] = _dot(y, w_ref[:, q_cols + kv_cols:]).astype(BF16)


def _qkv_project(stream, mod, g_pre, w_qkv, gains, cos_tab, sin_tab, geo, n_q_heads, n_kv_heads):
    nt, d = stream.shape
    tm = geo["tm"]
    grp = geo["grp"]
    n_lat, tps = geo["n_lat"], geo["tps"]
    hd = ATTN_HEAD_DIM
    q_cols, kv_cols = n_q_heads * hd, n_kv_heads * hd
    rope_blk = lambda i: jnp.where(i < n_lat, i % tps, tps)
    return pl.pallas_call(
        functools.partial(_qkv_kernel, n_q_heads=n_q_heads, n_kv_heads=n_kv_heads, q_scale=hd ** -0.5),
        grid=(nt // tm,),
        in_specs=[
            pl.BlockSpec((tm, d), lambda i: (i, 0)),
            pl.BlockSpec((None, 3, d), lambda i: (grp(i), 0, 0)),
            pl.BlockSpec((1, d), lambda i: (0, 0)),
            pl.BlockSpec((d, q_cols + 2 * kv_cols), lambda i: (0, 0)),
            pl.BlockSpec((2, hd), lambda i: (0, 0)),
            pl.BlockSpec((tm, hd), lambda i: (rope_blk(i), 0)),
            pl.BlockSpec((tm, hd), lambda i: (rope_blk(i), 0)),
        ],
        out_specs=[
            pl.BlockSpec((tm, q_cols), lambda i: (i, 0)),
            pl.BlockSpec((kv_cols, tm), lambda i: (0, i)),
            pl.BlockSpec((tm, kv_cols), lambda i: (i, 0)),
        ],
        out_shape=[
            jax.ShapeDtypeStruct((nt, q_cols), BF16),
            jax.ShapeDtypeStruct((kv_cols, nt), BF16),
            jax.ShapeDtypeStruct((nt, kv_cols), BF16),
        ],
        compiler_params=_cparams(("parallel",)),
        name="qkv_project",
    )(stream, mod, g_pre.reshape(1, d), w_qkv, gains, cos_tab, sin_tab)


def _attn_kernel(*refs, src_lens, tk):
    q_ref = refs[0]
    o_ref = refs[-1]
    hd = ATTN_HEAD_DIM
    tq = q_ref.shape[0]
    q = q_ref[...]
    qs = jnp.concatenate([q[:, g * hd:(g + 1) * hd] for g in range(ATTN_GROUP)], axis=0)
    rows = ATTN_GROUP * tq
    m = jnp.full((rows, 1), -jnp.inf, F32)
    l = jnp.zeros((rows, 1), F32)
    acc = jnp.zeros((rows, hd), F32)
    for s_i, n_keys in enumerate(src_lens):
        kt_ref, v_ref = refs[1 + 2 * s_i], refs[2 + 2 * s_i]
        step = min(tk, n_keys)
        for c in range(n_keys // step):
            s = _dot(qs, kt_ref[:, c * step:(c + 1) * step])
            m_new = jnp.maximum(m, jnp.max(s, axis=-1, keepdims=True))
            alpha = jnp.exp(m - m_new)
            p = jnp.exp(s - m_new)
            l = alpha * l + jnp.sum(p, axis=-1, keepdims=True)
            acc = alpha * acc + _dot(p.astype(BF16), v_ref[c * step:(c + 1) * step, :])
            m = m_new
    o = acc * (1.0 / l)
    o_ref[...] = jnp.concatenate([o[g * tq:(g + 1) * tq] for g in range(ATTN_GROUP)], axis=1).astype(BF16)


def _attention(q, kt, v, o_prev, *, batch, n_kv_heads, q_row0, n_q, srcs, tq):
    hd = ATTN_HEAD_DIM
    gw = ATTN_GROUP * hd
    nt = q.shape[0]
    q_blk0 = q_row0 // tq
    in_specs = [pl.BlockSpec((tq, gw), lambda b, h, i: (q_blk0 + b * (n_q // tq) + i, h))]
    args = [q]
    for row0, n_keys in srcs:
        blk0 = row0 // n_keys
        in_specs.append(pl.BlockSpec((hd, n_keys), lambda b, h, i, blk0=blk0: (h, blk0 + b)))
        in_specs.append(pl.BlockSpec((n_keys, hd), lambda b, h, i, blk0=blk0: (blk0 + b, h)))
        args += [kt, v]
    aliases = {}
    if o_prev is not None:
        in_specs.append(pl.BlockSpec(memory_space=pl.ANY))
        args.append(o_prev)
        aliases = {len(args) - 1: 0}
    kern = functools.partial(_attn_kernel, src_lens=tuple(n for _, n in srcs), tk=512)
    if o_prev is not None:
        inner = kern
        kern = lambda *refs: inner(*refs[:-2], refs[-1])
    return pl.pallas_call(
        kern,
        grid=(batch, n_kv_heads, n_q // tq),
        in_specs=in_specs,
        out_specs=pl.BlockSpec((tq, gw), lambda b, h, i: (q_blk0 + b * (n_q // tq) + i, h)),
        out_shape=jax.ShapeDtypeStruct((nt, n_kv_heads * gw), BF16),
        input_output_aliases=aliases,
        compiler_params=_cparams(("parallel", "parallel", "arbitrary")),
        name="attention",
    )(*args)


def _modnorm_kernel(x_ref, mod_ref, gpre_ref, y_ref):
    y_ref[...] = _modulate(x_ref[...], gpre_ref[...], mod_ref[0:1, :], mod_ref[1:2, :])


def _modnorm(stream, mod, g_pre, geo):
    nt, d = stream.shape
    tm = geo["tm"]
    grp = geo["grp"]
    return pl.pallas_call(
        _modnorm_kernel,
        grid=(nt // tm,),
        in_specs=[
            pl.BlockSpec((tm, d), lambda i: (i, 0)),
            pl.BlockSpec((None, 3, d), lambda i: (grp(i), 0, 0)),
            pl.BlockSpec((1, d), lambda i: (0, 0)),
        ],
        out_specs=pl.BlockSpec((tm, d), lambda i: (i, 0)),
        out_shape=jax.ShapeDtypeStruct((nt, d), F32),
        compiler_params=_cparams(("parallel",)),
        name="modnorm",
    )(stream, mod, g_pre.reshape(1, d))


def _head_sums(x, e_ref):
    hi, lo = _split2(x)
    e = e_ref[...]
    return _dot(hi, e) + _dot(lo, e)


def _rwkv_prep_kernel(y_ref, yp_ref, yn_ref, mu_ref, w_ref, dw1_ref, ia1_ref, g1_ref, dw2_ref, ia2_ref, g2_ref,
                      vec_ref, e_ref,
                      r_ref, v_ref, kk_ref, g_ref, lw_ref, kd_ref, bb_ref,
                      xr_scr, xk_scr, xv_scr, hw_scr, ha_scr, hg_scr, *, n_lat_rows, seq_lat, seq_ctx):
    i = pl.program_id(0)
    j = pl.program_id(1)
    tm = y_ref.shape[0]

    @pl.when(j == 0)
    def _():
        y = y_ref[...]
        row = i * tm + lax.broadcasted_iota(jnp.int32, (tm, 1), 0)
        is_lat = row < n_lat_rows
        pos = jnp.where(is_lat, row % seq_lat, (row - n_lat_rows) % seq_ctx)
        seq_len = jnp.where(is_lat, seq_lat, seq_ctx)
        loc = lax.broadcasted_iota(jnp.int32, (tm, 1), 0)
        up = jnp.where(loc == 0, yp_ref[7:8, :], pltpu.roll(y, 1, 0))
        up = jnp.where(pos == 0, 0.0, up)
        dn = jnp.where(loc == tm - 1, yn_ref[0:1, :], pltpu.roll(y, tm - 1, 0))
        dn = jnp.where(pos == seq_len - 1, 0.0, dn)
        xx = 0.5 * (up + dn) - y
        xr_scr[...] = (y + xx * mu_ref[0:1, :]).astype(BF16)
        xk_scr[...] = (y + xx * mu_ref[2:3, :]).astype(BF16)
        xv_scr[...] = (y + xx * mu_ref[3:4, :]).astype(BF16)
        xw = (y + xx * mu_ref[1:2, :]).astype(BF16)
        xa = (y + xx * mu_ref[4:5, :]).astype(BF16)
        xg = (y + xx * mu_ref[5:6, :]).astype(BF16)
        for dr in range(2):
            hw_scr[dr] = jnp.tanh(_dot(xw, dw1_ref[dr])).astype(BF16)
            ha_scr[dr] = _dot(xa, ia1_ref[dr]).astype(BF16)
        hg_scr[...] = jax.nn.sigmoid(_dot(xg, g1_ref[...])).astype(BF16)

    r_ref[...] = _dot(xr_scr[...], w_ref[0])
    k = _dot(xk_scr[...], w_ref[1])
    v_ref[...] = _dot(xv_scr[...], w_ref[2])
    g_ref[...] = _dot(hg_scr[...], g2_ref[...])
    kx = k * vec_ref[4:5, :]
    kk = kx * lax.rsqrt(_head_sums(kx * kx, e_ref) + L2N_EPS)
    kk_ref[...] = kk
    k_a = vec_ref[5:6, :]
    for dr in range(2):
        w_pre = vec_ref[dr:dr + 1, :] + _dot(hw_scr[dr], dw2_ref[dr])
        lw_ref[dr] = -DECAY_SCALE * jax.nn.sigmoid(w_pre)
        a = jax.nn.sigmoid(vec_ref[2 + dr:3 + dr, :] + _dot(ha_scr[dr], ia2_ref[dr]))
        kd_ref[dr] = k * (1.0 + (a - 1.0) * k_a)
        bb_ref[dr] = kk * a


def _rwkv_prep(y, p, geo):
    nt, d = y.shape
    tm = geo["tm_prep"]
    tn = _largest_divisor(d, (512, 256, 128))
    lo_w = p["dec_w1"].shape[-1]
    lo_a = p["icl_a1"].shape[-1]
    lo_g = p["g1"].shape[-1]
    hb = tm // 8
    n_blk8 = nt // 8
    const3 = lambda i, j: (0, 0, 0)
    col3 = lambda i, j: (0, 0, j)
    tok = pl.BlockSpec((tm, tn), lambda i, j: (i, j))
    tok2 = pl.BlockSpec((2, tm, tn), lambda i, j: (0, i, j))
    kern = functools.partial(_rwkv_prep_kernel, n_lat_rows=geo["n_lat_rows"], seq_lat=geo["S"], seq_ctx=geo["L"])
    return pl.pallas_call(
        kern,
        grid=(nt // tm, d // tn),
        in_specs=[
            pl.BlockSpec((tm, d), lambda i, j: (i, 0)),
            pl.BlockSpec((8, d), lambda i, j: (jnp.maximum(i * hb - 1, 0), 0)),
            pl.BlockSpec((8, d), lambda i, j: (jnp.minimum((i + 1) * hb, n_blk8 - 1), 0)),
            pl.BlockSpec((8, d), lambda i, j: (0, 0)),
            pl.BlockSpec((3, d, tn), col3),
            pl.BlockSpec((2, d, lo_w), const3),
            pl.BlockSpec((2, d, lo_a), const3),
            pl.BlockSpec((d, lo_g), lambda i, j: (0, 0)),
            pl.BlockSpec((2, lo_w, tn), col3),
            pl.BlockSpec((2, lo_a, tn), col3),
            pl.BlockSpec((lo_g, tn), lambda i, j: (0, j)),
            pl.BlockSpec((8, tn), lambda i, j: (0, j)),
            pl.BlockSpec((tn, tn), lambda i, j: (0, 0)),
        ],
        out_specs=[tok, tok, tok, tok, tok2, tok2, tok2],
        out_shape=[jax.ShapeDtypeStruct((nt, d), F32)] * 4 + [jax.ShapeDtypeStruct((2, nt, d), F32)] * 3,
        scratch_shapes=[pltpu.VMEM((tm, d), BF16)] * 3 + [
            pltpu.VMEM((2, tm, lo_w), BF16), pltpu.VMEM((2, tm, lo_a), BF16), pltpu.VMEM((tm, lo_g), BF16)],
        compiler_params=_cparams(("parallel", "arbitrary")),
        name="rwkv_prep",
    )(y, y, y, p["mu8"], p["w_rkv"], p["dec_w1"], p["icl_a1"], p["g1"], p["dec_w2"], p["icl_a2"], p["g2"],
      p["vecs"], p["head_ones"])


def _scan_group(r, v, kk, lw, kd, bb, state, masks):
    tri, bd_strict, bd_incl, bd_state, eye, m0, m1 = masks
    c = r.shape[0]
    lw_h = lw.astype(BF16)
    res = lw - lw_h.astype(F32)
    lw_m = res.astype(BF16)
    lw_l = (res - lw_m.astype(F32)).astype(BF16)
    cum = _dot(tri, lw_h) + (_dot(tri, lw_m) + _dot(tri, lw_l))
    tot = jnp.sum(lw, axis=0, keepdims=True)
    e_neg = jnp.exp(-cum)
    e_rem = jnp.exp(tot - cum)
    rt = r * jnp.exp(cum)
    kkt = kk * jnp.exp(cum - lw)
    stack = lambda a: jnp.concatenate([a, a], axis=0)
    stack_m = lambda a: jnp.concatenate([a * m0, a * m1], axis=0)
    unstack = lambda a: a[:c] * m0 + a[c:] * m1
    kk2 = _split2(stack_m(kkt))
    r2 = _split2(stack_m(rt))
    kh2 = _split2(stack(kd * e_neg))
    bh2 = _split2(stack(bb * e_neg))
    a_kkk = jnp.where(bd_strict, _mm3(kk2, kh2, nt=True), 0.0)
    a_kkb = jnp.where(bd_strict, _mm3(kk2, bh2, nt=True), 0.0)
    a_rk = jnp.where(bd_incl, _mm3(r2, kh2, nt=True), 0.0)
    a_rb = jnp.where(bd_incl, _mm3(r2, bh2, nt=True), 0.0)
    pw = _split2(-a_kkb)
    t_inv = eye - a_kkb
    n_sq = int(math.log2(c)) - 1
    for it in range(n_sq):
        sq = _mm3(pw, pw)
        pw = _split2(sq)
        t_inv = t_inv + _mm3(_split2(t_inv), pw)
    st = _split2(state)
    v2 = _split2(stack(v))
    w_in = _mm3(_split2(kkt), st, nt=True)
    x2 = stack(w_in) + _mm3(_split2(a_kkk), v2)
    u2 = _mm3(_split2(t_inv), _split2(x2))
    u = unstack(u2)
    y2 = _mm3(_split2(a_rk), v2) - _mm3(_split2(a_rb), _split2(u2))
    y = _mm3(_split2(rt), st, nt=True) + unstack(y2)
    vu_t = jnp.concatenate([v, -u], axis=0).T
    ds = _mm3(_split2(vu_t), _split2(jnp.concatenate([kd * e_rem, bb * e_rem], axis=0)))
    new_state = state * jnp.exp(tot) + jnp.where(bd_state, ds, 0.0)
    return y, new_state


def _scan_kernel(r_ref, v_ref, kk_ref, lw_ref, kd_ref, bb_ref, y_ref, state_scr):
    dr = pl.program_id(2)
    t = pl.program_id(3)
    c = r_ref.shape[0]
    n_grp = r_ref.shape[1] // LANES
    hd = RWKV_HEAD_DIM

    @pl.when(t == 0)
    def _():
        state_scr[...] = jnp.zeros_like(state_scr)

    ri = lax.broadcasted_iota(jnp.int32, (2 * c, 2 * c), 0)
    ci = lax.broadcasted_iota(jnp.int32, (2 * c, 2 * c), 1)
    sgn = 1 - 2 * dr
    order = jnp.where((ri // c) == (ci // c), (ri - ci) * sgn, -1)
    bd_strict = order > 0
    bd_incl = order >= 0
    eye = jnp.where(ri == ci, 1.0, 0.0).astype(F32)
    rc = lax.broadcasted_iota(jnp.int32, (c, c), 0)
    cc = lax.broadcasted_iota(jnp.int32, (c, c), 1)
    tri = jnp.where((rc - cc) * sgn >= 0, 1.0, 0.0).astype(BF16)
    sr = lax.broadcasted_iota(jnp.int32, (LANES, LANES), 0)
    sc = lax.broadcasted_iota(jnp.int32, (LANES, LANES), 1)
    bd_state = (sr // hd) == (sc // hd)
    lane = lax.broadcasted_iota(jnp.int32, (1, LANES), 1)
    m0 = jnp.where(lane < hd, 1.0, 0.0).astype(F32)
    m1 = 1.0 - m0
    masks = (tri, bd_strict, bd_incl, bd_state, eye, m0, m1)
    for gi in range(n_grp):
        sl = slice(gi * LANES, (gi + 1) * LANES)
        y, new_state = _scan_group(r_ref[:, sl], v_ref[:, sl], kk_ref[:, sl], lw_ref[:, sl], kd_ref[:, sl],
                                   bb_ref[:, sl], state_scr[gi], masks)
        y_ref[:, sl] = y
        state_scr[gi] = new_state


def _rwkv_scan(r, v, kk, lw, kd, bb, geo):
    nt, d = r.shape
    c = SCAN_CHUNK
    b, s_len, l_len = geo["B"], geo["S"], geo["L"]
    n_lc, n_sc = l_len // c, s_len // c
    wl = SCAN_LANES if d % SCAN_LANES == 0 else LANES
    ctx_blk0 = (b * s_len) // c

    def row_blk(bi, dr, t):
        t_ctx = jnp.where(dr == 0, t, n_lc - 1 - t)
        t_lat = jnp.where(dr == 0, t - n_lc, n_sc - 1 - (t - n_lc))
        return jnp.where(t < n_lc, ctx_blk0 + bi * n_lc + t_ctx, bi * n_sc + t_lat)

    shared = pl.BlockSpec((c, wl), lambda bi, h, dr, t: (row_blk(bi, dr, t), h))
    per_dir = pl.BlockSpec((None, c, wl), lambda bi, h, dr, t: (dr, row_blk(bi, dr, t), h))
    return pl.pallas_call(
        _scan_kernel,
        grid=(b, d // wl, 2, n_lc + n_sc),
        in_specs=[shared, shared, shared, per_dir, per_dir, per_dir],
        out_specs=per_dir,
        out_shape=jax.ShapeDtypeStruct((2, nt, d), F32),
        scratch_shapes=[pltpu.VMEM((wl // LANES, LANES, LANES), F32)],
        compiler_params=_cparams(("parallel", "parallel", "parallel", "arbitrary")),
        name="rwkv_scan",
    )(r, v, kk, lw, kd, bb)


def _rwkv_post_kernel(ys_ref, r_ref, v_ref, kd_ref, g_ref, vec_ref, e_ref, o_ref):
    inv_n = 1.0 / RWKV_HEAD_DIM
    o = ys_ref[0] + ys_ref[1]
    mean = _head_sums(o, e_ref) * inv_n
    oc = o - mean
    var = _head_sums(oc * oc, e_ref) * inv_n
    o = oc * lax.rsqrt(var + LNX_EPS) * vec_ref[1:2, :] + vec_ref[2:3, :]
    bonus = _head_sums(r_ref[...] * (kd_ref[0] + kd_ref[1]) * vec_ref[0:1, :], e_ref)
    o = o + bonus * v_ref[...]
    o_ref[...] = (o * g_ref[...]).astype(BF16)


def _rwkv_post(ys, r, v, kd, g, p, geo, rows_out):
    nt, d = r.shape
    tm = geo["tm"]
    tn = _largest_divisor(d, (512, 256, 128))
    tok = pl.BlockSpec((tm, tn), lambda i, j: (i, j))
    tok2 = pl.BlockSpec((2, tm, tn), lambda i, j: (0, i, j))
    return pl.pallas_call(
        _rwkv_post_kernel,
        grid=(rows_out // tm, d // tn),
        in_specs=[tok2, tok, tok, tok2, tok,
                  pl.BlockSpec((8, tn), lambda i, j: (0, j)),
                  pl.BlockSpec((tn, tn), lambda i, j: (0, 0))],
        out_specs=tok,
        out_shape=jax.ShapeDtypeStruct((rows_out, d), BF16),
        compiler_params=_cparams(("parallel", "parallel")),
        name="rwkv_post",
    )(ys, r, v, kd, g, p["post_vecs"], p["head_ones"])


def _pad_rows(a, rows):
    return jnp.concatenate([a, jnp.zeros((rows - a.shape[0],) + a.shape[1:], a.dtype)], axis=0)


def _deinterleave_cols(w, n_heads):
    d = w.shape[0]
    return w.reshape(d, n_heads, ATTN_HEAD_DIM // 2, 2).swapaxes(2, 3).reshape(d, n_heads * ATTN_HEAD_DIM)


def _rope_tables(n_tokens, tm):
    rows = n_tokens // GRID_W
    row = jnp.repeat(jnp.arange(rows, dtype=F32), GRID_W)
    col = jnp.tile(jnp.arange(GRID_W, dtype=F32), rows)
    n_freq = ATTN_HEAD_DIM // 4
    inv = ROPE_THETA ** (-jnp.arange(n_freq, dtype=F32) / n_freq)
    ang = jnp.concatenate([row[:, None] * inv, col[:, None] * inv], axis=-1)
    cos, sin = jnp.cos(ang), jnp.sin(ang)
    cos_tab = jnp.concatenate([cos, cos], axis=-1)
    sin_tab = jnp.concatenate([-sin, sin], axis=-1)
    cos_tab = jnp.concatenate([cos_tab, jnp.ones((tm, ATTN_HEAD_DIM), F32)], axis=0)
    sin_tab = jnp.concatenate([sin_tab, jnp.zeros((tm, ATTN_HEAD_DIM), F32)], axis=0)
    return cos_tab, sin_tab


def kernel(x, c, ctx, c_ctx, ada_w, ada_b, norm_pre, norm_post, ffn_w_in, ffn_w_out, attn_w_qkv, attn_w_o, attn_q_gain, attn_k_gain, rwkv_mu, rwkv_w_rkv, rwkv_w_o, rwkv_g1, rwkv_g2, rwkv_dec_w0, rwkv_dec_w1, rwkv_dec_w2, rwkv_icl_a0, rwkv_icl_a1, rwkv_icl_a2, rwkv_k_k, rwkv_k_a, rwkv_r_k, rwkv_lnx_w, rwkv_lnx_b):
    b, s_len, d = x.shape
    l_len = ctx.shape[1]
    depth = ada_w.shape[0]
    n_lat_rows, n_ctx_rows = b * s_len, b * l_len
    nt = n_lat_rows + n_ctx_rows
    tm = _largest_divisor(math.gcd(s_len, n_ctx_rows), (512, 256, 128, 64, 32, 16, 8))
    tm_prep = min(tm, 256)
    n_lat, tps = n_lat_rows // tm, s_len // tm
    geo = dict(B=b, S=s_len, L=l_len, tm=tm, tm_prep=tm_prep, n_lat=n_lat, tps=tps, n_lat_rows=n_lat_rows,
               grp=lambda i: jnp.where(i < n_lat, 1 + i // tps, 0))
    assert s_len % SCAN_CHUNK == 0 and l_len % SCAN_CHUNK == 0 and n_lat_rows % l_len == 0
    n_q_heads = d // ATTN_HEAD_DIM
    n_kv_heads = n_q_heads // ATTN_GROUP
    q_cols, kv_cols = n_q_heads * ATTN_HEAD_DIM, n_kv_heads * ATTN_HEAD_DIM
    tq = min(128, l_len)

    stream = jnp.concatenate([x.reshape(n_lat_rows, d), ctx.reshape(n_ctx_rows, d)], axis=0)
    mod_rows = 8 * (-(-(1 + b) // 8))
    s_all = _pad_rows(jnp.concatenate([c_ctx[None, :], c], axis=0), mod_rows)
    mods = _ada_mods(s_all, ada_w, ada_b).reshape(depth, mod_rows, 9, d)

    ffn_w_in_b = ffn_w_in.astype(BF16)
    ffn_w_out_b = ffn_w_out.astype(BF16)
    cos_tab, sin_tab = _rope_tables(s_len, tm)
    tn_heads = _largest_divisor(d, (512, 256, 128))
    idx = jnp.arange(tn_heads) // RWKV_HEAD_DIM
    head_ones = (idx[:, None] == idx[None, :]).astype(BF16)

    for i in range(depth):
        last = i == depth - 1
        j = i // 2
        mod = lambda sub: mods[i, :, 3 * sub:3 * sub + 3]
        rows_after = n_lat_rows if last else nt
        stream = _ffn_half(stream, mod(0), norm_pre[i, 0], norm_post[i, 0], ffn_w_in_b[i, 0], ffn_w_out_b[i, 0],
                           geo, nt)
        if i % 2 == 0:
            w = attn_w_qkv[j]
            w_perm = jnp.concatenate([
                _deinterleave_cols(w[:, :q_cols], n_q_heads),
                _deinterleave_cols(w[:, q_cols:q_cols + kv_cols], n_kv_heads),
                w[:, q_cols + kv_cols:]], axis=1).astype(BF16)
            deint = lambda g: g.reshape(ATTN_HEAD_DIM // 2, 2).T.reshape(ATTN_HEAD_DIM)
            gains = jnp.stack([deint(attn_q_gain[j]), deint(attn_k_gain[j])])
            q, kt, v = _qkv_project(stream, mod(1), norm_pre[i, 1], w_perm, gains, cos_tab, sin_tab, geo,
                                    n_q_heads, n_kv_heads)
            o = _attention(q, kt, v, None, batch=b, n_kv_heads=n_kv_heads, q_row0=0, n_q=s_len,
                           srcs=[(n_lat_rows, l_len), (0, s_len)], tq=tq)
            if not last:
                o = _attention(q, kt, v, o, batch=b, n_kv_heads=n_kv_heads, q_row0=n_lat_rows, n_q=l_len,
                               srcs=[(n_lat_rows, l_len)], tq=tq)
            w_o = attn_w_o[j].astype(BF16)
        else:
            p = dict(
                mu8=_pad_rows(rwkv_mu[j], 8),
                w_rkv=rwkv_w_rkv[j].astype(BF16),
                dec_w1=rwkv_dec_w1[j].astype(BF16), dec_w2=rwkv_dec_w2[j].astype(BF16),
                icl_a1=rwkv_icl_a1[j].astype(BF16), icl_a2=rwkv_icl_a2[j].astype(BF16),
                g1=rwkv_g1[j].astype(BF16), g2=rwkv_g2[j].astype(BF16),
                vecs=_pad_rows(jnp.concatenate([rwkv_dec_w0[j], rwkv_icl_a0[j], rwkv_k_k[j][None], rwkv_k_a[j][None]],
                                               axis=0), 8),
                post_vecs=_pad_rows(jnp.stack([rwkv_r_k[j].reshape(d), rwkv_lnx_w[j], rwkv_lnx_b[j]]), 8),
                head_ones=head_ones,
            )
            y = _modnorm(stream, mod(1), norm_pre[i, 1], geo)
            r, v, kk, g, lw, kd, bb = _rwkv_prep(y, p, geo)
            ys = _rwkv_scan(r, v, kk, lw, kd, bb, geo)
            o = _rwkv_post(ys, r, v, kd, g, p, geo, rows_after)
            w_o = rwkv_w_o[j].astype(BF16)
        stream = _oproj_residual(o, stream, mod(1), norm_post[i, 1], w_o, geo, rows_after)
        stream = _ffn_half(stream, mod(2), norm_pre[i, 2], norm_post[i, 2], ffn_w_in_b[i, 1], ffn_w_out_b[i, 1],
                           geo, rows_after)
    return stream.reshape(b, s_len, d)
```

```python
import functools
import math

import jax
import jax.numpy as jnp
from jax import lax
from jax.experimental import pallas as pl
from jax.experimental.pallas import tpu as pltpu

F32 = jnp.float32
BF16 = jnp.bfloat16

NORM_EPS = 1e-6
LNX_EPS = 64e-5
L2N_EPS = 1e-12
ATTN_HEAD_DIM = 128
ATTN_GROUP = 4
ROPE_THETA = 10000.0
GRID_W = 64
RWKV_HEAD_DIM = 64
LANES = 128
SCAN_CHUNK = 64
SCAN_LANES = 1024
VMEM_LIMIT = 56 * 1024 * 1024
DECAY_SCALE = math.exp(-0.5)


def _cparams(sem):
    return pltpu.CompilerParams(dimension_semantics=sem, vmem_limit_bytes=VMEM_LIMIT)


def _largest_divisor(n, candidates):
    for c in candidates:
        if n % c == 0:
            return c
    raise ValueError(f"no tile in {candidates} divides {n}")


def _dot(a, b):
    return jnp.dot(a, b, preferred_element_type=F32)


def _dot_nt(a, b):
    return lax.dot_general(a, b, (((1,), (1,)), ((), ())), preferred_element_type=F32)


def _split2(a):
    hi = a.astype(BF16)
    lo = (a - hi.astype(F32)).astype(BF16)
    return hi, lo


def _mm3(a, b, nt=False):
    d = _dot_nt if nt else _dot
    return d(a[0], b[0]) + (d(a[0], b[1]) + d(a[1], b[0]))


def _rms_rows(x, eps):
    return x * lax.rsqrt(jnp.mean(x * x, axis=-1, keepdims=True) + eps)


def _modulate(x, g_pre, shift, scale):
    return _rms_rows(x, NORM_EPS) * g_pre * (1.0 + scale) + shift


def _ada_kernel(s_ref, w_ref, b_ref, o_ref):
    s = s_ref[...]
    s = s * jax.nn.sigmoid(s)
    o_ref[...] = _mm3(_split2(s), _split2(w_ref[...])) + b_ref[...]


def _ada_mods(s_all, ada_w, ada_b):
    depth, d, n = ada_w.shape
    tn = _largest_divisor(n, (1024, 512, 256, 128))
    rows = s_all.shape[0]
    return pl.pallas_call(
        _ada_kernel,
        grid=(depth, n // tn),
        in_specs=[
            pl.BlockSpec((rows, d), lambda i, j: (0, 0)),
            pl.BlockSpec((None, d, tn), lambda i, j: (i, 0, j)),
            pl.BlockSpec((None, 1, tn), lambda i, j: (i, 0, j)),
        ],
        out_specs=pl.BlockSpec((None, rows, tn), lambda i, j: (i, 0, j)),
        out_shape=jax.ShapeDtypeStruct((depth, rows, n), F32),
        compiler_params=_cparams(("arbitrary", "arbitrary")),
        name="ada_mods",
    )(s_all, ada_w, ada_b.reshape(depth, 1, n))


def _ffn_kernel(x_ref, mod_ref, gpre_ref, gpost_ref, wg_ref, wu_ref, wo_ref, o_ref, y_scr, acc_scr):
    f = pl.program_id(1)

    @pl.when(f == 0)
    def _():
        y = _modulate(x_ref[...], gpre_ref[...], mod_ref[0:1, :], mod_ref[1:2, :])
        y_scr[...] = y.astype(BF16)
        acc_scr[...] = jnp.zeros_like(acc_scr)

    y = y_scr[...]
    g = _dot(y, wg_ref[...])
    u = _dot(y, wu_ref[...])
    a = (g * jax.nn.sigmoid(g)) * u
    acc_scr[...] += _dot(a.astype(BF16), wo_ref[...])

    @pl.when(f == pl.num_programs(1) - 1)
    def _():
        h = _rms_rows(acc_scr[...], NORM_EPS) * gpost_ref[...]
        o_ref[...] = x_ref[...] + 0.5 * (mod_ref[2:3, :] * h)


def _ffn_half(stream, mod, g_pre, g_post, w_in, w_out, geo, rows_out):
    nt, d = stream.shape
    f_dim = w_out.shape[0]
    tm = geo["tm"]
    tf = _largest_divisor(f_dim, (512, 256, 128))
    nf = f_dim // tf
    grp = geo["grp"]
    return pl.pallas_call(
        _ffn_kernel,
        grid=(rows_out // tm, nf),
        in_specs=[
            pl.BlockSpec((tm, d), lambda i, f: (i, 0)),
            pl.BlockSpec((None, 3, d), lambda i, f: (grp(i), 0, 0)),
            pl.BlockSpec((1, d), lambda i, f: (0, 0)),
            pl.BlockSpec((1, d), lambda i, f: (0, 0)),
            pl.BlockSpec((d, tf), lambda i, f: (0, f)),
            pl.BlockSpec((d, tf), lambda i, f: (0, nf + f)),
            pl.BlockSpec((tf, d), lambda i, f: (f, 0)),
        ],
        out_specs=pl.BlockSpec((tm, d), lambda i, f: (i, 0)),
        out_shape=jax.ShapeDtypeStruct((rows_out, d), F32),
        scratch_shapes=[pltpu.VMEM((tm, d), BF16), pltpu.VMEM((tm, d), F32)],
        compiler_params=_cparams(("parallel", "arbitrary")),
        name="ffn_half",
    )(stream, mod, g_pre.reshape(1, d), g_post.reshape(1, d), w_in, w_in, w_out)


def _oproj_kernel(o_ref, x_ref, mod_ref, gpost_ref, w_ref, out_ref):
    h = _dot(o_ref[...], w_ref[...])
    h = _rms_rows(h, NORM_EPS) * gpost_ref[...]
    out_ref[...] = x_ref[...] + mod_ref[2:3, :] * h


def _oproj_residual(o, stream, mod, g_post, w_o, geo, rows_out):
    d = stream.shape[1]
    tm = geo["tm"]
    grp = geo["grp"]
    return pl.pallas_call(
        _oproj_kernel,
        grid=(rows_out // tm,),
        in_specs=[
            pl.BlockSpec((tm, d), lambda i: (i, 0)),
            pl.BlockSpec((tm, d), lambda i: (i, 0)),
            pl.BlockSpec((None, 3, d), lambda i: (grp(i), 0, 0)),
            pl.BlockSpec((1, d), lambda i: (0, 0)),
            pl.BlockSpec((d, d), lambda i: (0, 0)),
        ],
        out_specs=pl.BlockSpec((tm, d), lambda i: (i, 0)),
        out_shape=jax.ShapeDtypeStruct((rows_out, d), F32),
        compiler_params=_cparams(("parallel",)),
        name="oproj_residual",
    )(o, stream, mod, g_post.reshape(1, d), w_o)


def _qkv_kernel(x_ref, mod_ref, gpre_ref, w_ref, gains_ref, cos_ref, sin_ref, q_ref, kt_ref, v_ref,
                *, n_q_heads, n_kv_heads, q_scale):
    hd = ATTN_HEAD_DIM
    y = _modulate(x_ref[...], gpre_ref[...], mod_ref[0:1, :], mod_ref[1:2, :]).astype(BF16)
    cos = cos_ref[...]
    sin = sin_ref[...]

    def head_post(h, gain):
        hn = _rms_rows(h, NORM_EPS) * gain
        return hn * cos + pltpu.roll(hn, hd // 2, 1) * sin

    for c in range(n_q_heads // ATTN_GROUP):
        w = ATTN_GROUP * hd
        acc = _dot(y, w_ref[:, c * w:(c + 1) * w])
        for g in range(ATTN_GROUP):
            h = head_post(acc[:, g * hd:(g + 1) * hd], gains_ref[0:1, :]) * q_scale
            q_ref[:, (c * ATTN_GROUP + g) * hd:(c * ATTN_GROUP + g + 1) * hd] = h.astype(BF16)
    q_cols = n_q_heads * hd
    kv_cols = n_kv_heads * hd
    acc = _dot(y, w_ref[:, q_cols:q_cols + kv_cols])
    for h_i in range(n_kv_heads):
        kh = head_post(acc[:, h_i * hd:(h_i + 1) * hd], gains_ref[1:2, :])
        kt_ref[h_i * hd:(h_i + 1) * hd, :] = kh.T.astype(BF16)
    v_ref[...] = _dot(y, w_ref[:, q_cols + kv_cols:]).astype(BF16)


def _qkv_project(stream, mod, g_pre, w_qkv, gains, cos_tab, sin_tab, geo, n_q_heads, n_kv_heads):
    nt, d = stream.shape
    tm = geo["tm"]
    grp = geo["grp"]
    n_lat, tps = geo["n_lat"], geo["tps"]
    hd = ATTN_HEAD_DIM
    q_cols, kv_cols = n_q_heads * hd, n_kv_heads * hd
    rope_blk = lambda i: jnp.where(i < n_lat, i % tps, tps)
    return pl.pallas_call(
        functools.partial(_qkv_kernel, n_q_heads=n_q_heads, n_kv_heads=n_kv_heads, q_scale=hd ** -0.5),
        grid=(nt // tm,),
        in_specs=[
            pl.BlockSpec((tm, d), lambda i: (i, 0)),
            pl.BlockSpec((None, 3, d), lambda i: (grp(i), 0, 0)),
            pl.BlockSpec((1, d), lambda i: (0, 0)),
            pl.BlockSpec((d, q_cols + 2 * kv_cols), lambda i: (0, 0)),
            pl.BlockSpec((2, hd), lambda i: (0, 0)),
            pl.BlockSpec((tm, hd), lambda i: (rope_blk(i), 0)),
            pl.BlockSpec((tm, hd), lambda i: (rope_blk(i), 0)),
        ],
        out_specs=[
            pl.BlockSpec((tm, q_cols), lambda i: (i, 0)),
            pl.BlockSpec((kv_cols, tm), lambda i: (0, i)),
            pl.BlockSpec((tm, kv_cols), lambda i: (i, 0)),
        ],
        out_shape=[
            jax.ShapeDtypeStruct((nt, q_cols), BF16),
            jax.ShapeDtypeStruct((kv_cols, nt), BF16),
            jax.ShapeDtypeStruct((nt, kv_cols), BF16),
        ],
        compiler_params=_cparams(("parallel",)),
        name="qkv_project",
    )(stream, mod, g_pre.reshape(1, d), w_qkv, gains, cos_tab, sin_tab)


def _attn_kernel(*refs, src_lens, tk):
    q_ref = refs[0]
    o_ref = refs[-1]
    hd = ATTN_HEAD_DIM
    tq = q_ref.shape[0]
    q = q_ref[...]
    qs = jnp.concatenate([q[:, g * hd:(g + 1) * hd] for g in range(ATTN_GROUP)], axis=0)
    rows = ATTN_GROUP * tq
    m = jnp.full((rows, 1), -jnp.inf, F32)
    l = jnp.zeros((rows, 1), F32)
    acc = jnp.zeros((rows, hd), F32)
    for s_i, n_keys in enumerate(src_lens):
        kt_ref, v_ref = refs[1 + 2 * s_i], refs[2 + 2 * s_i]
        step = min(tk, n_keys)
        for c in range(n_keys // step):
            s = _dot(qs, kt_ref[:, c * step:(c + 1) * step])
            m_new = jnp.maximum(m, jnp.max(s, axis=-1, keepdims=True))
            alpha = jnp.exp(m - m_new)
            p = jnp.exp(s - m_new)
            l = alpha * l + jnp.sum(p, axis=-1, keepdims=True)
            acc = alpha * acc + _dot(p.astype(BF16), v_ref[c * step:(c + 1) * step, :])
            m = m_new
    o = acc * (1.0 / l)
    o_ref[...] = jnp.concatenate([o[g * tq:(g + 1) * tq] for g in range(ATTN_GROUP)], axis=1).astype(BF16)


def _attention(q, kt, v, o_prev, *, batch, n_kv_heads, q_row0, n_q, srcs, tq):
    hd = ATTN_HEAD_DIM
    gw = ATTN_GROUP * hd
    nt = q.shape[0]
    q_blk0 = q_row0 // tq
    in_specs = [pl.BlockSpec((tq, gw), lambda b, h, i: (q_blk0 + b * (n_q // tq) + i, h))]
    args = [q]
    for row0, n_keys in srcs:
        blk0 = row0 // n_keys
        in_specs.append(pl.BlockSpec((hd, n_keys), lambda b, h, i, blk0=blk0: (h, blk0 + b)))
        in_specs.append(pl.BlockSpec((n_keys, hd), lambda b, h, i, blk0=blk0: (blk0 + b, h)))
        args += [kt, v]
    aliases = {}
    if o_prev is not None:
        in_specs.append(pl.BlockSpec(memory_space=pl.ANY))
        args.append(o_prev)
        aliases = {len(args) - 1: 0}
    kern = functools.partial(_attn_kernel, src_lens=tuple(n for _, n in srcs), tk=512)
    if o_prev is not None:
        inner = kern
        kern = lambda *refs: inner(*refs[:-2], refs[-1])
    return pl.pallas_call(
        kern,
        grid=(batch, n_kv_heads, n_q // tq),
        in_specs=in_specs,
        out_specs=pl.BlockSpec((tq, gw), lambda b, h, i: (q_blk0 + b * (n_q // tq) + i, h)),
        out_shape=jax.ShapeDtypeStruct((nt, n_kv_heads * gw), BF16),
        input_output_aliases=aliases,
        compiler_params=_cparams(("parallel", "parallel", "arbitrary")),
        name="attention",
    )(*args)


def _modnorm_kernel(x_ref, mod_ref, gpre_ref, y_ref):
    y_ref[...] = _modulate(x_ref[...], gpre_ref[...], mod_ref[0:1, :], mod_ref[1:2, :])


def _modnorm(stream, mod, g_pre, geo):
    nt, d = stream.shape
    tm = geo["tm"]
    grp = geo["grp"]
    return pl.pallas_call(
        _modnorm_kernel,
        grid=(nt // tm,),
        in_specs=[
            pl.BlockSpec((tm, d), lambda i: (i, 0)),
            pl.BlockSpec((None, 3, d), lambda i: (grp(i), 0, 0)),
            pl.BlockSpec((1, d), lambda i: (0, 0)),
        ],
        out_specs=pl.BlockSpec((tm, d), lambda i: (i, 0)),
        out_shape=jax.ShapeDtypeStruct((nt, d), F32),
        compiler_params=_cparams(("parallel",)),
        name="modnorm",
    )(stream, mod, g_pre.reshape(1, d))


def _head_sums(x, e_ref):
    hi, lo = _split2(x)
    e = e_ref[...]
    return _dot(hi, e) + _dot(lo, e)


def _rwkv_prep_kernel(y_ref, yp_ref, yn_ref, mu_ref, w_ref, dw1_ref, ia1_ref, g1_ref, dw2_ref, ia2_ref, g2_ref,
                      vec_ref, e_ref,
                      r_ref, v_ref, kk_ref, g_ref, lw_ref, kd_ref, bb_ref,
                      xr_scr, xk_scr, xv_scr, hw_scr, ha_scr, hg_scr, *, n_lat_rows, seq_lat, seq_ctx):
    i = pl.program_id(0)
    j = pl.program_id(1)
    tm = y_ref.shape[0]

    @pl.when(j == 0)
    def _():
        y = y_ref[...]
        row = i * tm + lax.broadcasted_iota(jnp.int32, (tm, 1), 0)
        is_lat = row < n_lat_rows
        pos = jnp.where(is_lat, row % seq_lat, (row - n_lat_rows) % seq_ctx)
        seq_len = jnp.where(is_lat, seq_lat, seq_ctx)
        loc = lax.broadcasted_iota(jnp.int32, (tm, 1), 0)
        up = jnp.where(loc == 0, yp_ref[7:8, :], pltpu.roll(y, 1, 0))
        up = jnp.where(pos == 0, 0.0, up)
        dn = jnp.where(loc == tm - 1, yn_ref[0:1, :], pltpu.roll(y, tm - 1, 0))
        dn = jnp.where(pos == seq_len - 1, 0.0, dn)
        xx = 0.5 * (up + dn) - y
        xr_scr[...] = (y + xx * mu_ref[0:1, :]).astype(BF16)
        xk_scr[...] = (y + xx * mu_ref[2:3, :]).astype(BF16)
        xv_scr[...] = (y + xx * mu_ref[3:4, :]).astype(BF16)
        xw = (y + xx * mu_ref[1:2, :]).astype(BF16)
        xa = (y + xx * mu_ref[4:5, :]).astype(BF16)
        xg = (y + xx * mu_ref[5:6, :]).astype(BF16)
        for dr in range(2):
            hw_scr[dr] = jnp.tanh(_dot(xw, dw1_ref[dr])).astype(BF16)
            ha_scr[dr] = _dot(xa, ia1_ref[dr]).astype(BF16)
        hg_scr[...] = jax.nn.sigmoid(_dot(xg, g1_ref[...])).astype(BF16)

    r_ref[...] = _dot(xr_scr[...], w_ref[0])
    k = _dot(xk_scr[...], w_ref[1])
    v_ref[...] = _dot(xv_scr[...], w_ref[2])
    g_ref[...] = _dot(hg_scr[...], g2_ref[...])
    kx = k * vec_ref[4:5, :]
    kk = kx * lax.rsqrt(_head_sums(kx * kx, e_ref) + L2N_EPS)
    kk_ref[...] = kk
    k_a = vec_ref[5:6, :]
    for dr in range(2):
        w_pre = vec_ref[dr:dr + 1, :] + _dot(hw_scr[dr], dw2_ref[dr])
        lw_ref[dr] = -DECAY_SCALE * jax.nn.sigmoid(w_pre)
        a = jax.nn.sigmoid(vec_ref[2 + dr:3 + dr, :] + _dot(ha_scr[dr], ia2_ref[dr]))
        kd_ref[dr] = k * (1.0 + (a - 1.0) * k_a)
        bb_ref[dr] = kk * a


def _rwkv_prep(y, p, geo):
    nt, d = y.shape
    tm = geo["tm_prep"]
    tn = _largest_divisor(d, (512, 256, 128))
    lo_w = p["dec_w1"].shape[-1]
    lo_a = p["icl_a1"].shape[-1]
    lo_g = p["g1"].shape[-1]
    hb = tm // 8
    n_blk8 = nt // 8
    const3 = lambda i, j: (0, 0, 0)
    col3 = lambda i, j: (0, 0, j)
    tok = pl.BlockSpec((tm, tn), lambda i, j: (i, j))
    tok2 = pl.BlockSpec((2, tm, tn), lambda i, j: (0, i, j))
    kern = functools.partial(_rwkv_prep_kernel, n_lat_rows=geo["n_lat_rows"], seq_lat=geo["S"], seq_ctx=geo["L"])
    return pl.pallas_call(
        kern,
        grid=(nt // tm, d // tn),
        in_specs=[
            pl.BlockSpec((tm, d), lambda i, j: (i, 0)),
            pl.BlockSpec((8, d), lambda i, j: (jnp.maximum(i * hb - 1, 0), 0)),
            pl.BlockSpec((8, d), lambda i, j: (jnp.minimum((i + 1) * hb, n_blk8 - 1), 0)),
            pl.BlockSpec((8, d), lambda i, j: (0, 0)),
            pl.BlockSpec((3, d, tn), col3),
            pl.BlockSpec((2, d, lo_w), const3),
            pl.BlockSpec((2, d, lo_a), const3),
            pl.BlockSpec((d, lo_g), lambda i, j: (0, 0)),
            pl.BlockSpec((2, lo_w, tn), col3),
            pl.BlockSpec((2, lo_a, tn), col3),
            pl.BlockSpec((lo_g, tn), lambda i, j: (0, j)),
            pl.BlockSpec((8, tn), lambda i, j: (0, j)),
            pl.BlockSpec((tn, tn), lambda i, j: (0, 0)),
        ],
        out_specs=[tok, tok, tok, tok, tok2, tok2, tok2],
        out_shape=[jax.ShapeDtypeStruct((nt, d), F32)] * 4 + [jax.ShapeDtypeStruct((2, nt, d), F32)] * 3,
        scratch_shapes=[pltpu.VMEM((tm, d), BF16)] * 3 + [
            pltpu.VMEM((2, tm, lo_w), BF16), pltpu.VMEM((2, tm, lo_a), BF16), pltpu.VMEM((tm, lo_g), BF16)],
        compiler_params=_cparams(("parallel", "arbitrary")),
        name="rwkv_prep",
    )(y, y, y, p["mu8"], p["w_rkv"], p["dec_w1"], p["icl_a1"], p["g1"], p["dec_w2"], p["icl_a2"], p["g2"],
      p["vecs"], p["head_ones"])


def _scan_chunk(r, v, kk, lw, kd, bb, state, masks):
    tri, bd_strict, bd_incl, bd_state, eye, m0, m1, sib = masks
    c = r[0].shape[0]
    c2 = 2 * c
    each = lambda fn, *cols: [fn(*args) for args in zip(*cols)]
    unstack = lambda a: a[:c] * m0 + a[c:] * m1

    def split3(x):
        hi = x.astype(BF16)
        res = x - hi.astype(F32)
        mid = res.astype(BF16)
        return jnp.concatenate([hi, mid, (res - mid.astype(F32)).astype(BF16)], axis=1)

    cum3 = each(lambda x: _dot(tri, split3(x)), lw)
    cum = each(lambda x: x[:, :LANES] + (x[:, LANES:2 * LANES] + x[:, 2 * LANES:]), cum3)
    tot = each(lambda x: jnp.sum(x, axis=0, keepdims=True), lw)
    e_neg = each(lambda x: jnp.exp(-x), cum)
    rt = each(lambda x, cm: x * jnp.exp(cm), r, cum)
    kkt = each(lambda x, cm, w: x * jnp.exp(cm - w), kk, cum, lw)
    lhs = each(lambda k_, r_: jnp.concatenate([(k_ * m0).astype(BF16), (k_ * m1).astype(BF16),
                                               (r_ * m0).astype(BF16), (r_ * m1).astype(BF16)], axis=0), kkt, rt)

    def make_rhs(kd_, bb_, en):
        kh = (kd_ * en).astype(BF16)
        bh = (bb_ * en).astype(BF16)
        return jnp.concatenate([kh, kh, bh, bh], axis=0)

    rhs = each(make_rhs, kd, bb, e_neg)
    a_all = each(_dot_nt, lhs, rhs)
    a_kkb = each(lambda a: jnp.where(bd_strict, a[:c2, c2:], 0.0), a_all)
    t_inv = each(lambda a: eye - jnp.where(sib[0], a, 0.0), a_kkb)
    for lvl in range(1, len(sib)):
        def merge(t, a, lvl=lvl):
            tb = t.astype(BF16)
            return t - _dot(_dot(tb, jnp.where(sib[lvl], a, 0.0).astype(BF16)).astype(BF16), tb)
        t_inv = each(merge, t_inv, a_kkb)
    st = each(lambda s: s.astype(BF16), state)
    v2 = each(lambda x: jnp.concatenate([x.astype(BF16)] * 2, axis=0), v)
    x2 = each(lambda l_, s_, a, v_: _dot_nt(l_[:c2], s_) + _dot(jnp.where(bd_strict, a[:c2, :c2], 0.0).astype(BF16), v_),
              lhs, st, a_all, v2)
    u2 = each(lambda t, x: _dot(t.astype(BF16), x.astype(BF16)), t_inv, x2)

    def outputs(l_, s_, a, v_, u_):
        a_rk = jnp.where(bd_incl, a[c2:, :c2], 0.0).astype(BF16)
        a_rb_neg = jnp.where(bd_incl, -a[c2:, c2:], 0.0).astype(BF16)
        y2 = _dot_nt(l_[c2:], s_) + _dot(jnp.concatenate([a_rk, a_rb_neg], axis=1),
                                         jnp.concatenate([v_, u_.astype(BF16)], axis=0))
        return unstack(y2)

    y = each(outputs, lhs, st, a_all, v2, u2)

    def state_update(s_, v_, u_, kd_, bb_, cm, tt):
        e_rem = jnp.exp(tt - cm)
        vu_t = jnp.concatenate([v_, -unstack(u_)], axis=0).T.astype(BF16)
        ds = _dot(vu_t, jnp.concatenate([(kd_ * e_rem).astype(BF16), (bb_ * e_rem).astype(BF16)], axis=0))
        return s_ * jnp.exp(tt) + jnp.where(bd_state, ds, 0.0)

    new_state = each(state_update, state, v, u2, kd, bb, cum, tot)
    return y, new_state


def _scan_kernel(r_ref, v_ref, kk_ref, lw_ref, kd_ref, bb_ref, y_ref, state_scr):
    dr = pl.program_id(2)
    t = pl.program_id(3)
    c = r_ref.shape[0]
    n_grp = r_ref.shape[1] // LANES
    hd = RWKV_HEAD_DIM

    @pl.when(t == 0)
    def _():
        state_scr[...] = jnp.zeros_like(state_scr)

    ri = lax.broadcasted_iota(jnp.int32, (2 * c, 2 * c), 0)
    ci = lax.broadcasted_iota(jnp.int32, (2 * c, 2 * c), 1)
    sgn = 1 - 2 * dr
    order = jnp.where((ri // c) == (ci // c), (ri - ci) * sgn, -1)
    bd_strict = order > 0
    bd_incl = order >= 0
    eye = jnp.where(ri == ci, 1.0, 0.0).astype(F32)
    rc = lax.broadcasted_iota(jnp.int32, (c, c), 0)
    cc = lax.broadcasted_iota(jnp.int32, (c, c), 1)
    tri = jnp.where((rc - cc) * sgn >= 0, 1.0, 0.0).astype(BF16)
    sr = lax.broadcasted_iota(jnp.int32, (LANES, LANES), 0)
    sc = lax.broadcasted_iota(jnp.int32, (LANES, LANES), 1)
    bd_state = (sr // hd) == (sc // hd)
    lane = lax.broadcasted_iota(jnp.int32, (1, LANES), 1)
    m0 = jnp.where(lane < hd, 1.0, 0.0).astype(F32)
    m1 = 1.0 - m0
    blk = lambda k: (ri >> k) == (ci >> k)
    sib = [blk(1)] + [blk(k + 1) & jnp.logical_not(blk(k)) for k in range(1, int(math.log2(c)))]
    masks = (tri, bd_strict, bd_incl, bd_state, eye, m0, m1, sib)
    lanes = [slice(gi * LANES, (gi + 1) * LANES) for gi in range(n_grp)]
    cols = lambda ref: [ref[:, sl] for sl in lanes]
    y, new_state = _scan_chunk(cols(r_ref), cols(v_ref), cols(kk_ref), cols(lw_ref), cols(kd_ref), cols(bb_ref),
                               [state_scr[gi] for gi in range(n_grp)], masks)
    for gi in range(n_grp):
        y_ref[:, lanes[gi]] = y[gi]
        state_scr[gi] = new_state[gi]


def _rwkv_scan(r, v, kk, lw, kd, bb, geo):
    nt, d = r.shape
    c = SCAN_CHUNK
    b, s_len, l_len = geo["B"], geo["S"], geo["L"]
    n_lc, n_sc = l_len // c, s_len // c
    wl = SCAN_LANES if d % SCAN_LANES == 0 else LANES
    ctx_blk0 = (b * s_len) // c

    def row_blk(bi, dr, t):
        t_ctx = jnp.where(dr == 0, t, n_lc - 1 - t)
        t_lat = jnp.where(dr == 0, t - n_lc, n_sc - 1 - (t - n_lc))
        return jnp.where(t < n_lc, ctx_blk0 + bi * n_lc + t_ctx, bi * n_sc + t_lat)

    shared = pl.BlockSpec((c, wl), lambda bi, h, dr, t: (row_blk(bi, dr, t), h))
    per_dir = pl.BlockSpec((None, c, wl), lambda bi, h, dr, t: (dr, row_blk(bi, dr, t), h))
    return pl.pallas_call(
        _scan_kernel,
        grid=(b, d // wl, 2, n_lc + n_sc),
        in_specs=[shared, shared, shared, per_dir, per_dir, per_dir],
        out_specs=per_dir,
        out_shape=jax.ShapeDtypeStruct((2, nt, d), F32),
        scratch_shapes=[pltpu.VMEM((wl // LANES, LANES, LANES), F32)],
        compiler_params=_cparams(("parallel", "parallel", "parallel", "arbitrary")),
        name="rwkv_scan",
    )(r, v, kk, lw, kd, bb)


def _rwkv_post_kernel(ys_ref, r_ref, v_ref, kd_ref, g_ref, vec_ref, e_ref, o_ref):
    inv_n = 1.0 / RWKV_HEAD_DIM
    o = ys_ref[0] + ys_ref[1]
    mean = _head_sums(o, e_ref) * inv_n
    oc = o - mean
    var = _head_sums(oc * oc, e_ref) * inv_n
    o = oc * lax.rsqrt(var + LNX_EPS) * vec_ref[1:2, :] + vec_ref[2:3, :]
    bonus = _head_sums(r_ref[...] * (kd_ref[0] + kd_ref[1]) * vec_ref[0:1, :], e_ref)
    o = o + bonus * v_ref[...]
    o_ref[...] = (o * g_ref[...]).astype(BF16)


def _rwkv_post(ys, r, v, kd, g, p, geo, rows_out):
    nt, d = r.shape
    tm = geo["tm"]
    tn = _largest_divisor(d, (512, 256, 128))
    tok = pl.BlockSpec((tm, tn), lambda i, j: (i, j))
    tok2 = pl.BlockSpec((2, tm, tn), lambda i, j: (0, i, j))
    return pl.pallas_call(
        _rwkv_post_kernel,
        grid=(rows_out // tm, d // tn),
        in_specs=[tok2, tok, tok, tok2, tok,
                  pl.BlockSpec((8, tn), lambda i, j: (0, j)),
                  pl.BlockSpec((tn, tn), lambda i, j: (0, 0))],
        out_specs=tok,
        out_shape=jax.ShapeDtypeStruct((rows_out, d), BF16),
        compiler_params=_cparams(("parallel", "parallel")),
        name="rwkv_post",
    )(ys, r, v, kd, g, p["post_vecs"], p["head_ones"])


def _pad_rows(a, rows):
    return jnp.concatenate([a, jnp.zeros((rows - a.shape[0],) + a.shape[1:], a.dtype)], axis=0)


def _deinterleave_cols(w, n_heads):
    d = w.shape[0]
    return w.reshape(d, n_heads, ATTN_HEAD_DIM // 2, 2).swapaxes(2, 3).reshape(d, n_heads * ATTN_HEAD_DIM)


def _rope_tables(n_tokens, tm):
    rows = n_tokens // GRID_W
    row = jnp.repeat(jnp.arange(rows, dtype=F32), GRID_W)
    col = jnp.tile(jnp.arange(GRID_W, dtype=F32), rows)
    n_freq = ATTN_HEAD_DIM // 4
    inv = ROPE_THETA ** (-jnp.arange(n_freq, dtype=F32) / n_freq)
    ang = jnp.concatenate([row[:, None] * inv, col[:, None] * inv], axis=-1)
    cos, sin = jnp.cos(ang), jnp.sin(ang)
    cos_tab = jnp.concatenate([cos, cos], axis=-1)
    sin_tab = jnp.concatenate([-sin, sin], axis=-1)
    cos_tab = jnp.concatenate([cos_tab, jnp.ones((tm, ATTN_HEAD_DIM), F32)], axis=0)
    sin_tab = jnp.concatenate([sin_tab, jnp.zeros((tm, ATTN_HEAD_DIM), F32)], axis=0)
    return cos_tab, sin_tab


def kernel(x, c, ctx, c_ctx, ada_w, ada_b, norm_pre, norm_post, ffn_w_in, ffn_w_out, attn_w_qkv, attn_w_o, attn_q_gain, attn_k_gain, rwkv_mu, rwkv_w_rkv, rwkv_w_o, rwkv_g1, rwkv_g2, rwkv_dec_w0, rwkv_dec_w1, rwkv_dec_w2, rwkv_icl_a0, rwkv_icl_a1, rwkv_icl_a2, rwkv_k_k, rwkv_k_a, rwkv_r_k, rwkv_lnx_w, rwkv_lnx_b):
    b, s_len, d = x.shape
    l_len = ctx.shape[1]
    depth = ada_w.shape[0]
    n_lat_rows, n_ctx_rows = b * s_len, b * l_len
    nt = n_lat_rows + n_ctx_rows
    tm = _largest_divisor(math.gcd(s_len, n_ctx_rows), (512, 256, 128, 64, 32, 16, 8))
    tm_prep = min(tm, 256)
    n_lat, tps = n_lat_rows // tm, s_len // tm
    geo = dict(B=b, S=s_len, L=l_len, tm=tm, tm_prep=tm_prep, n_lat=n_lat, tps=tps, n_lat_rows=n_lat_rows,
               grp=lambda i: jnp.where(i < n_lat, 1 + i // tps, 0))
    assert s_len % SCAN_CHUNK == 0 and l_len % SCAN_CHUNK == 0 and n_lat_rows % l_len == 0
    n_q_heads = d // ATTN_HEAD_DIM
    n_kv_heads = n_q_heads // ATTN_GROUP
    q_cols, kv_cols = n_q_heads * ATTN_HEAD_DIM, n_kv_heads * ATTN_HEAD_DIM
    tq = min(128, l_len)

    stream = jnp.concatenate([x.reshape(n_lat_rows, d), ctx.reshape(n_ctx_rows, d)], axis=0)
    mod_rows = 8 * (-(-(1 + b) // 8))
    s_all = _pad_rows(jnp.concatenate([c_ctx[None, :], c], axis=0), mod_rows)
    mods = _ada_mods(s_all, ada_w, ada_b).reshape(depth, mod_rows, 9, d)

    ffn_w_in_b = ffn_w_in.astype(BF16)
    ffn_w_out_b = ffn_w_out.astype(BF16)
    cos_tab, sin_tab = _rope_tables(s_len, tm)
    tn_heads = _largest_divisor(d, (512, 256, 128))
    idx = jnp.arange(tn_heads) // RWKV_HEAD_DIM
    head_ones = (idx[:, None] == idx[None, :]).astype(BF16)

    for i in range(depth):
        last = i == depth - 1
        j = i // 2
        mod = lambda sub: mods[i, :, 3 * sub:3 * sub + 3]
        rows_after = n_lat_rows if last else nt
        stream = _ffn_half(stream, mod(0), norm_pre[i, 0], norm_post[i, 0], ffn_w_in_b[i, 0], ffn_w_out_b[i, 0],
                           geo, nt)
        if i % 2 == 0:
            w = attn_w_qkv[j]
            w_perm = jnp.concatenate([
                _deinterleave_cols(w[:, :q_cols], n_q_heads),
                _deinterleave_cols(w[:, q_cols:q_cols + kv_cols], n_kv_heads),
                w[:, q_cols + kv_cols:]], axis=1).astype(BF16)
            deint = lambda g: g.reshape(ATTN_HEAD_DIM // 2, 2).T.reshape(ATTN_HEAD_DIM)
            gains = jnp.stack([deint(attn_q_gain[j]), deint(attn_k_gain[j])])
            q, kt, v = _qkv_project(stream, mod(1), norm_pre[i, 1], w_perm, gains, cos_tab, sin_tab, geo,
                                    n_q_heads, n_kv_heads)
            o = _attention(q, kt, v, None, batch=b, n_kv_heads=n_kv_heads, q_row0=0, n_q=s_len,
                           srcs=[(n_lat_rows, l_len), (0, s_len)], tq=tq)
            if not last:
                o = _attention(q, kt, v, o, batch=b, n_kv_heads=n_kv_heads, q_row0=n_lat_rows, n_q=l_len,
                               srcs=[(n_lat_rows, l_len)], tq=tq)
            w_o = attn_w_o[j].astype(BF16)
        else:
            p = dict(
                mu8=_pad_rows(rwkv_mu[j], 8),
                w_rkv=rwkv_w_rkv[j].astype(BF16),
                dec_w1=rwkv_dec_w1[j].astype(BF16), dec_w2=rwkv_dec_w2[j].astype(BF16),
                icl_a1=rwkv_icl_a1[j].astype(BF16), icl_a2=rwkv_icl_a2[j].astype(BF16),
                g1=rwkv_g1[j].astype(BF16), g2=rwkv_g2[j].astype(BF16),
                vecs=_pad_rows(jnp.concatenate([rwkv_dec_w0[j], rwkv_icl_a0[j], rwkv_k_k[j][None], rwkv_k_a[j][None]],
                                               axis=0), 8),
                post_vecs=_pad_rows(jnp.stack([rwkv_r_k[j].reshape(d), rwkv_lnx_w[j], rwkv_lnx_b[j]]), 8),
                head_ones=head_ones,
            )
            y = _modnorm(stream, mod(1), norm_pre[i, 1], geo)
            r, v, kk, g, lw, kd, bb = _rwkv_prep(y, p, geo)
            ys = _rwkv_scan(r, v, kk, lw, kd, bb, geo)
            o = _rwkv_post(ys, r, v, kd, g, p, geo, rows_after)
            w_o = rwkv_w_o[j].astype(BF16)
        stream = _oproj_residual(o, stream, mod(1), norm_post[i, 1], w_o, geo, rows_after)
        stream = _ffn_half(stream, mod(2), norm_pre[i, 2], norm_post[i, 2], ffn_w_in_b[i, 1], ffn_w_out_b[i, 1],
                           geo, rows_after)
    return stream.reshape(b, s_len, d)
```

```python
import functools
import math

import jax
import jax.numpy as jnp
from jax import lax
from jax.experimental import pallas as pl
from jax.experimental.pallas import tpu as pltpu

F32 = jnp.float32
BF16 = jnp.bfloat16

NORM_EPS = 1e-6
LNX_EPS = 64e-5
L2N_EPS = 1e-12
ATTN_HEAD_DIM = 128
ATTN_GROUP = 4
ROPE_THETA = 10000.0
GRID_W = 64
RWKV_HEAD_DIM = 64
LANES = 128
SCAN_CHUNK = 64
SCAN_LANES = 2048
VMEM_LIMIT = 56 * 1024 * 1024
DECAY_SCALE = math.exp(-0.5)


def _cparams(sem):
    return pltpu.CompilerParams(dimension_semantics=sem, vmem_limit_bytes=VMEM_LIMIT)


def _largest_divisor(n, candidates):
    for c in candidates:
        if n % c == 0:
            return c
    raise ValueError(f"no tile in {candidates} divides {n}")


def _dot(a, b):
    return jnp.dot(a, b, preferred_element_type=F32)


def _dot_nt(a, b):
    return lax.dot_general(a, b, (((1,), (1,)), ((), ())), preferred_element_type=F32)


def _split2(a):
    hi = a.astype(BF16)
    lo = (a - hi.astype(F32)).astype(BF16)
    return hi, lo


def _mm3(a, b, nt=False):
    d = _dot_nt if nt else _dot
    return d(a[0], b[0]) + (d(a[0], b[1]) + d(a[1], b[0]))


def _rms_rows(x, eps):
    return x * lax.rsqrt(jnp.mean(x * x, axis=-1, keepdims=True) + eps)


def _modulate(x, g_pre, shift, scale):
    return _rms_rows(x, NORM_EPS) * g_pre * (1.0 + scale) + shift


def _ada_kernel(s_ref, w_ref, b_ref, o_ref):
    s = s_ref[...]
    s = s * jax.nn.sigmoid(s)
    o_ref[...] = _mm3(_split2(s), _split2(w_ref[...])) + b_ref[...]


def _ada_mods(s_all, ada_w, ada_b):
    depth, d, n = ada_w.shape
    tn = _largest_divisor(n, (1024, 512, 256, 128))
    rows = s_all.shape[0]
    return pl.pallas_call(
        _ada_kernel,
        grid=(depth, n // tn),
        in_specs=[
            pl.BlockSpec((rows, d), lambda i, j: (0, 0)),
            pl.BlockSpec((None, d, tn), lambda i, j: (i, 0, j)),
            pl.BlockSpec((None, 1, tn), lambda i, j: (i, 0, j)),
        ],
        out_specs=pl.BlockSpec((None, rows, tn), lambda i, j: (i, 0, j)),
        out_shape=jax.ShapeDtypeStruct((depth, rows, n), F32),
        compiler_params=_cparams(("arbitrary", "arbitrary")),
        name="ada_mods",
    )(s_all, ada_w, ada_b.reshape(depth, 1, n))


def _ffn_kernel(x_ref, mod_ref, gpre_ref, gpost_ref, wg_ref, wu_ref, wo_ref, o_ref, y_scr):
    f = pl.program_id(1)

    @pl.when(f == 0)
    def _():
        y = _modulate(x_ref[...], gpre_ref[...], mod_ref[0:1, :], mod_ref[1:2, :])
        y_scr[...] = y.astype(BF16)
        o_ref[...] = jnp.zeros_like(o_ref)

    y = y_scr[...]
    g = _dot(y, wg_ref[...])
    u = _dot(y, wu_ref[...])
    a = (g * jax.nn.sigmoid(g)) * u
    o_ref[...] += _dot(a.astype(BF16), wo_ref[...])

    @pl.when(f == pl.num_programs(1) - 1)
    def _():
        h = _rms_rows(o_ref[...], NORM_EPS) * gpost_ref[...]
        o_ref[...] = x_ref[...] + 0.5 * (mod_ref[2:3, :] * h)


def _ffn_half(stream, mod, g_pre, g_post, w_in, w_out, layer, half, geo, rows_out):
    nt, d = stream.shape
    f_dim = w_out.shape[2]
    tm = geo["tm_ffn"]
    tf = _largest_divisor(f_dim, (512, 256, 128))
    nf = f_dim // tf
    grp = geo["grp_for"](tm)
    return pl.pallas_call(
        _ffn_kernel,
        grid=(rows_out // tm, nf),
        in_specs=[
            pl.BlockSpec((tm, d), lambda i, f: (i, 0)),
            pl.BlockSpec((None, 3, d), lambda i, f: (grp(i), 0, 0)),
            pl.BlockSpec((1, d), lambda i, f: (0, 0)),
            pl.BlockSpec((1, d), lambda i, f: (0, 0)),
            pl.BlockSpec((None, None, d, tf), lambda i, f: (layer, half, 0, f)),
            pl.BlockSpec((None, None, d, tf), lambda i, f: (layer, half, 0, nf + f)),
            pl.BlockSpec((None, None, tf, d), lambda i, f: (layer, half, f, 0)),
        ],
        out_specs=pl.BlockSpec((tm, d), lambda i, f: (i, 0)),
        out_shape=jax.ShapeDtypeStruct((rows_out, d), F32),
        scratch_shapes=[pltpu.VMEM((tm, d), BF16)],
        compiler_params=_cparams(("parallel", "arbitrary")),
        name="ffn_half",
    )(stream, mod, g_pre.reshape(1, d), g_post.reshape(1, d), w_in, w_in, w_out)


def _oproj_kernel(o_ref, x_ref, mod_ref, gpost_ref, w_ref, out_ref):
    h = _dot(o_ref[...], w_ref[...])
    h = _rms_rows(h, NORM_EPS) * gpost_ref[...]
    out_ref[...] = x_ref[...] + mod_ref[2:3, :] * h


def _oproj_residual(o, stream, mod, g_post, w_o, geo, rows_out):
    d = stream.shape[1]
    tm = geo["tm"]
    grp = geo["grp"]
    return pl.pallas_call(
        _oproj_kernel,
        grid=(rows_out // tm,),
        in_specs=[
            pl.BlockSpec((tm, d), lambda i: (i, 0)),
            pl.BlockSpec((tm, d), lambda i: (i, 0)),
            pl.BlockSpec((None, 3, d), lambda i: (grp(i), 0, 0)),
            pl.BlockSpec((1, d), lambda i: (0, 0)),
            pl.BlockSpec((d, d), lambda i: (0, 0)),
        ],
        out_specs=pl.BlockSpec((tm, d), lambda i: (i, 0)),
        out_shape=jax.ShapeDtypeStruct((rows_out, d), F32),
        compiler_params=_cparams(("parallel",)),
        name="oproj_residual",
    )(o, stream, mod, g_post.reshape(1, d), w_o)


def _qkv_kernel(x_ref, mod_ref, gpre_ref, w_ref, gains_ref, cos_ref, sin_ref, q_ref, kt_ref, v_ref,
                *, n_q_heads, n_kv_heads, q_scale):
    hd = ATTN_HEAD_DIM
    y = _modulate(x_ref[...], gpre_ref[...], mod_ref[0:1, :], mod_ref[1:2, :]).astype(BF16)
    cos = cos_ref[...]
    sin = sin_ref[...]

    def head_post(h, gain):
        hn = _rms_rows(h, NORM_EPS) * gain
        return hn * cos + pltpu.roll(hn, hd // 2, 1) * sin

    for c in range(n_q_heads // ATTN_GROUP):
        w = ATTN_GROUP * hd
        acc = _dot(y, w_ref[:, c * w:(c + 1) * w])
        for g in range(ATTN_GROUP):
            h = head_post(acc[:, g * hd:(g + 1) * hd], gains_ref[0:1, :]) * q_scale
            q_ref[:, (c * ATTN_GROUP + g) * hd:(c * ATTN_GROUP + g + 1) * hd] = h.astype(BF16)
    q_cols = n_q_heads * hd
    kv_cols = n_kv_heads * hd
    acc = _dot(y, w_ref[:, q_cols:q_cols + kv_cols])
    for h_i in range(n_kv_heads):
        kh = head_post(acc[:, h_i * hd:(h_i + 1) * hd], gains_ref[1:2, :])
        kt_ref[h_i * hd:(h_i + 1) * hd, :] = kh.T.astype(BF16)
    v_ref[...] = _dot(y, w_ref[:, q_cols + kv_cols:]).astype(BF16)


def _qkv_project(stream, mod, g_pre, w_qkv, gains, cos_tab, sin_tab, geo, n_q_heads, n_kv_heads):
    nt, d = stream.shape
    tm = geo["tm"]
    grp = geo["grp"]
    n_lat, tps = geo["n_lat"], geo["tps"]
    hd = ATTN_HEAD_DIM
    q_cols, kv_cols = n_q_heads * hd, n_kv_heads * hd
    rope_blk = lambda i: jnp.where(i < n_lat, i % tps, tps)
    return pl.pallas_call(
        functools.partial(_qkv_kernel, n_q_heads=n_q_heads, n_kv_heads=n_kv_heads,
                          q_scale=hd ** -0.5 * math.log2(math.e)),
        grid=(nt // tm,),
        in_specs=[
            pl.BlockSpec((tm, d), lambda i: (i, 0)),
            pl.BlockSpec((None, 3, d), lambda i: (grp(i), 0, 0)),
            pl.BlockSpec((1, d), lambda i: (0, 0)),
            pl.BlockSpec((d, q_cols + 2 * kv_cols), lambda i: (0, 0)),
            pl.BlockSpec((2, hd), lambda i: (0, 0)),
            pl.BlockSpec((tm, hd), lambda i: (rope_blk(i), 0)),
            pl.BlockSpec((tm, hd), lambda i: (rope_blk(i), 0)),
        ],
        out_specs=[
            pl.BlockSpec((tm, q_cols), lambda i: (i, 0)),
            pl.BlockSpec((kv_cols, tm), lambda i: (0, i)),
            pl.BlockSpec((tm, kv_cols), lambda i: (i, 0)),
        ],
        out_shape=[
            jax.ShapeDtypeStruct((nt, q_cols), BF16),
            jax.ShapeDtypeStruct((kv_cols, nt), BF16),
            jax.ShapeDtypeStruct((nt, kv_cols), BF16),
        ],
        compiler_params=_cparams(("parallel",)),
        name="qkv_project",
    )(stream, mod, g_pre.reshape(1, d), w_qkv, gains, cos_tab, sin_tab)


def _attn_kernel(*refs, src_lens, tk):
    q_ref = refs[0]
    o_ref = refs[-1]
    hd = ATTN_HEAD_DIM
    tq = q_ref.shape[0]
    q = q_ref[...]
    qs = jnp.concatenate([q[:, g * hd:(g + 1) * hd] for g in range(ATTN_GROUP)], axis=0)
    rows = ATTN_GROUP * tq
    m = jnp.full((rows, 1), -jnp.inf, F32)
    acc = jnp.zeros((rows, 2 * hd), F32)
    for s_i, n_keys in enumerate(src_lens):
        kt_ref, v_ref = refs[1 + 2 * s_i], refs[2 + 2 * s_i]
        step = min(tk, n_keys)
        ones = jnp.ones((step, hd), BF16)
        for c in range(n_keys // step):
            s = _dot(qs, kt_ref[:, c * step:(c + 1) * step])
            m_new = jnp.maximum(m, jnp.max(s, axis=-1, keepdims=True))
            p = jnp.exp2(s - m_new).astype(BF16)
            v_aug = jnp.concatenate([v_ref[c * step:(c + 1) * step, :], ones], axis=1)
            acc = jnp.exp2(m - m_new) * acc + _dot(p, v_aug)
            m = m_new
    o = acc[:, :hd] * (1.0 / acc[:, hd:])
    o_ref[...] = jnp.concatenate([o[g * tq:(g + 1) * tq] for g in range(ATTN_GROUP)], axis=1).astype(BF16)


def _attention(q, kt, v, o_prev, *, batch, n_kv_heads, q_row0, n_q, srcs, tq):
    hd = ATTN_HEAD_DIM
    gw = ATTN_GROUP * hd
    nt = q.shape[0]
    q_blk0 = q_row0 // tq
    in_specs = [pl.BlockSpec((tq, gw), lambda b, h, i: (q_blk0 + b * (n_q // tq) + i, h))]
    args = [q]
    for row0, n_keys in srcs:
        blk0 = row0 // n_keys
        in_specs.append(pl.BlockSpec((hd, n_keys), lambda b, h, i, blk0=blk0: (h, blk0 + b)))
        in_specs.append(pl.BlockSpec((n_keys, hd), lambda b, h, i, blk0=blk0: (blk0 + b, h)))
        args += [kt, v]
    aliases = {}
    if o_prev is not None:
        in_specs.append(pl.BlockSpec(memory_space=pl.ANY))
        args.append(o_prev)
        aliases = {len(args) - 1: 0}
    kern = functools.partial(_attn_kernel, src_lens=tuple(n for _, n in srcs), tk=256)
    if o_prev is not None:
        inner = kern
        kern = lambda *refs: inner(*refs[:-2], refs[-1])
    return pl.pallas_call(
        kern,
        grid=(batch, n_kv_heads, n_q // tq),
        in_specs=in_specs,
        out_specs=pl.BlockSpec((tq, gw), lambda b, h, i: (q_blk0 + b * (n_q // tq) + i, h)),
        out_shape=jax.ShapeDtypeStruct((nt, n_kv_heads * gw), BF16),
        input_output_aliases=aliases,
        compiler_params=_cparams(("parallel", "parallel", "arbitrary")),
        name="attention",
    )(*args)


def _modnorm_kernel(x_ref, mod_ref, gpre_ref, y_ref):
    y_ref[...] = _modulate(x_ref[...], gpre_ref[...], mod_ref[0:1, :], mod_ref[1:2, :])


def _modnorm(stream, mod, g_pre, geo):
    nt, d = stream.shape
    tm = geo["tm"]
    grp = geo["grp"]
    return pl.pallas_call(
        _modnorm_kernel,
        grid=(nt // tm,),
        in_specs=[
            pl.BlockSpec((tm, d), lambda i: (i, 0)),
            pl.BlockSpec((None, 3, d), lambda i: (grp(i), 0, 0)),
            pl.BlockSpec((1, d), lambda i: (0, 0)),
        ],
        out_specs=pl.BlockSpec((tm, d), lambda i: (i, 0)),
        out_shape=jax.ShapeDtypeStruct((nt, d), F32),
        compiler_params=_cparams(("parallel",)),
        name="modnorm",
    )(stream, mod, g_pre.reshape(1, d))


def _head_sums(x, e_ref):
    hi, lo = _split2(x)
    e = e_ref[...]
    return _dot(hi, e) + _dot(lo, e)


def _rwkv_prep_kernel(y_ref, yp_ref, yn_ref, mu_ref, w_ref, dw1_ref, ia1_ref, g1_ref, dw2_ref, ia2_ref, g2_ref,
                      vec_ref, e_ref,
                      r_ref, v_ref, kk_ref, g_ref, lw_ref, kd_ref, bb_ref,
                      xr_scr, xk_scr, xv_scr, hw_scr, ha_scr, hg_scr, *, n_lat_rows, seq_lat, seq_ctx):
    i = pl.program_id(0)
    j = pl.program_id(1)
    tm = y_ref.shape[0]

    @pl.when(j == 0)
    def _():
        y = y_ref[...]
        row = i * tm + lax.broadcasted_iota(jnp.int32, (tm, 1), 0)
        is_lat = row < n_lat_rows
        pos = jnp.where(is_lat, row % seq_lat, (row - n_lat_rows) % seq_ctx)
        seq_len = jnp.where(is_lat, seq_lat, seq_ctx)
        loc = lax.broadcasted_iota(jnp.int32, (tm, 1), 0)
        up = jnp.where(loc == 0, yp_ref[7:8, :], pltpu.roll(y, 1, 0))
        up = jnp.where(pos == 0, 0.0, up)
        dn = jnp.where(loc == tm - 1, yn_ref[0:1, :], pltpu.roll(y, tm - 1, 0))
        dn = jnp.where(pos == seq_len - 1, 0.0, dn)
        xx = 0.5 * (up + dn) - y
        xr_scr[...] = (y + xx * mu_ref[0:1, :]).astype(BF16)
        xk_scr[...] = (y + xx * mu_ref[2:3, :]).astype(BF16)
        xv_scr[...] = (y + xx * mu_ref[3:4, :]).astype(BF16)
        xw = (y + xx * mu_ref[1:2, :]).astype(BF16)
        xa = (y + xx * mu_ref[4:5, :]).astype(BF16)
        xg = (y + xx * mu_ref[5:6, :]).astype(BF16)
        for dr in range(2):
            hw_scr[dr] = jnp.tanh(_dot(xw, dw1_ref[dr])).astype(BF16)
            ha_scr[dr] = _dot(xa, ia1_ref[dr]).astype(BF16)
        hg_scr[...] = jax.nn.sigmoid(_dot(xg, g1_ref[...])).astype(BF16)

    r_ref[...] = _dot(xr_scr[...], w_ref[0])
    k = _dot(xk_scr[...], w_ref[1])
    v_ref[...] = _dot(xv_scr[...], w_ref[2])
    g_ref[...] = _dot(hg_scr[...], g2_ref[...])
    kx = k * vec_ref[4:5, :]
    kk = kx * lax.rsqrt(_head_sums(kx * kx, e_ref) + L2N_EPS)
    kk_ref[...] = kk
    k_a = vec_ref[5:6, :]
    for dr in range(2):
        w_pre = vec_ref[dr:dr + 1, :] + _dot(hw_scr[dr], dw2_ref[dr])
        lw_ref[dr] = -DECAY_SCALE * jax.nn.sigmoid(w_pre)
        a = jax.nn.sigmoid(vec_ref[2 + dr:3 + dr, :] + _dot(ha_scr[dr], ia2_ref[dr]))
        kd_ref[dr] = k * (1.0 + (a - 1.0) * k_a)
        bb_ref[dr] = kk * a


def _rwkv_prep(y, p, geo):
    nt, d = y.shape
    tm = geo["tm_prep"]
    tn = _largest_divisor(d, (512, 256, 128))
    lo_w = p["dec_w1"].shape[-1]
    lo_a = p["icl_a1"].shape[-1]
    lo_g = p["g1"].shape[-1]
    hb = tm // 8
    n_blk8 = nt // 8
    const3 = lambda i, j: (0, 0, 0)
    col3 = lambda i, j: (0, 0, j)
    tok = pl.BlockSpec((tm, tn), lambda i, j: (i, j))
    tok2 = pl.BlockSpec((2, tm, tn), lambda i, j: (0, i, j))
    kern = functools.partial(_rwkv_prep_kernel, n_lat_rows=geo["n_lat_rows"], seq_lat=geo["S"], seq_ctx=geo["L"])
    return pl.pallas_call(
        kern,
        grid=(nt // tm, d // tn),
        in_specs=[
            pl.BlockSpec((tm, d), lambda i, j: (i, 0)),
            pl.BlockSpec((8, d), lambda i, j: (jnp.maximum(i * hb - 1, 0), 0)),
            pl.BlockSpec((8, d), lambda i, j: (jnp.minimum((i + 1) * hb, n_blk8 - 1), 0)),
            pl.BlockSpec((8, d), lambda i, j: (0, 0)),
            pl.BlockSpec((3, d, tn), col3),
            pl.BlockSpec((2, d, lo_w), const3),
            pl.BlockSpec((2, d, lo_a), const3),
            pl.BlockSpec((d, lo_g), lambda i, j: (0, 0)),
            pl.BlockSpec((2, lo_w, tn), col3),
            pl.BlockSpec((2, lo_a, tn), col3),
            pl.BlockSpec((lo_g, tn), lambda i, j: (0, j)),
            pl.BlockSpec((8, tn), lambda i, j: (0, j)),
            pl.BlockSpec((tn, tn), lambda i, j: (0, 0)),
        ],
        out_specs=[tok, tok, tok, tok, tok2, tok2, tok2],
        out_shape=[jax.ShapeDtypeStruct((nt, d), F32)] * 4 + [jax.ShapeDtypeStruct((2, nt, d), F32)] * 3,
        scratch_shapes=[pltpu.VMEM((tm, d), BF16)] * 3 + [
            pltpu.VMEM((2, tm, lo_w), BF16), pltpu.VMEM((2, tm, lo_a), BF16), pltpu.VMEM((tm, lo_g), BF16)],
        compiler_params=_cparams(("parallel", "arbitrary")),
        name="rwkv_prep",
    )(y, y, y, p["mu8"], p["w_rkv"], p["dec_w1"], p["icl_a1"], p["g1"], p["dec_w2"], p["icl_a2"], p["g2"],
      p["vecs"], p["head_ones"])


def _scan_chunk(r, v, kk, lw, kd, bb, state, masks):
    tri, bd_strict, bd_incl, bd_state, eye, m0, m1, sib = masks
    c = r[0].shape[0]
    c2 = 2 * c
    each = lambda fn, *cols: [fn(*args) for args in zip(*cols)]
    unstack = lambda a: a[:c] * m0 + a[c:] * m1

    def split3(x):
        hi = x.astype(BF16)
        res = x - hi.astype(F32)
        mid = res.astype(BF16)
        return jnp.concatenate([hi, mid, (res - mid.astype(F32)).astype(BF16)], axis=1)

    cum3 = each(lambda x: _dot(tri, split3(x)), lw)
    cum = each(lambda x: x[:, :LANES] + (x[:, LANES:2 * LANES] + x[:, 2 * LANES:]), cum3)
    tot = each(lambda x: jnp.sum(x, axis=0, keepdims=True), lw)
    e_neg = each(lambda x: jnp.exp(-x), cum)
    rt = each(lambda x, cm: x * jnp.exp(cm), r, cum)
    kkt = each(lambda x, cm, w: x * jnp.exp(cm - w), kk, cum, lw)
    lhs = each(lambda k_, r_: jnp.concatenate([(k_ * m0).astype(BF16), (k_ * m1).astype(BF16),
                                               (r_ * m0).astype(BF16), (r_ * m1).astype(BF16)], axis=0), kkt, rt)

    def make_rhs(kd_, bb_, en):
        kh = (kd_ * en).astype(BF16)
        bh = (bb_ * en).astype(BF16)
        return jnp.concatenate([kh, kh, bh, bh], axis=0)

    rhs = each(make_rhs, kd, bb, e_neg)
    a_all = each(_dot_nt, lhs, rhs)
    a_kkb = each(lambda a: jnp.where(bd_strict, a[:c2, c2:], 0.0), a_all)
    t_inv = each(lambda a: eye - jnp.where(sib[0], a, 0.0), a_kkb)
    for lvl in range(1, len(sib)):
        def merge(t, a, lvl=lvl):
            tb = t.astype(BF16)
            return t - _dot(_dot(tb, jnp.where(sib[lvl], a, 0.0).astype(BF16)).astype(BF16), tb)
        t_inv = each(merge, t_inv, a_kkb)
    st = each(lambda s: s.astype(BF16), state)
    v2 = each(lambda x: jnp.concatenate([x.astype(BF16)] * 2, axis=0), v)
    x2 = each(lambda l_, s_, a, v_: _dot_nt(l_[:c2], s_) + _dot(jnp.where(bd_strict, a[:c2, :c2], 0.0).astype(BF16), v_),
              lhs, st, a_all, v2)
    u2 = each(lambda t, x: _dot(t.astype(BF16), x.astype(BF16)), t_inv, x2)

    def outputs(l_, s_, a, v_, u_):
        a_rk = jnp.where(bd_incl, a[c2:, :c2], 0.0).astype(BF16)
        a_rb_neg = jnp.where(bd_incl, -a[c2:, c2:], 0.0).astype(BF16)
        y2 = _dot_nt(l_[c2:], s_) + _dot(jnp.concatenate([a_rk, a_rb_neg], axis=1),
                                         jnp.concatenate([v_, u_.astype(BF16)], axis=0))
        return unstack(y2)

    y = each(outputs, lhs, st, a_all, v2, u2)

    def state_update(s_, v_, u_, kd_, bb_, cm, tt):
        e_rem = jnp.exp(tt - cm)
        vu_t = jnp.concatenate([v_, -unstack(u_)], axis=0).T.astype(BF16)
        ds = _dot(vu_t, jnp.concatenate([(kd_ * e_rem).astype(BF16), (bb_ * e_rem).astype(BF16)], axis=0))
        return s_ * jnp.exp(tt) + jnp.where(bd_state, ds, 0.0)

    new_state = each(state_update, state, v, u2, kd, bb, cum, tot)
    return y, new_state


def _scan_kernel(r_ref, v_ref, kk_ref, lw_ref, kd_ref, bb_ref, y_ref, state_scr):
    dr = pl.program_id(2)
    t = pl.program_id(3)
    c = r_ref.shape[0]
    n_grp = r_ref.shape[1] // LANES
    hd = RWKV_HEAD_DIM

    @pl.when(t == 0)
    def _():
        state_scr[...] = jnp.zeros_like(state_scr)

    ri = lax.broadcasted_iota(jnp.int32, (2 * c, 2 * c), 0)
    ci = lax.broadcasted_iota(jnp.int32, (2 * c, 2 * c), 1)
    sgn = 1 - 2 * dr
    order = jnp.where((ri // c) == (ci // c), (ri - ci) * sgn, -1)
    bd_strict = order > 0
    bd_incl = order >= 0
    eye = jnp.where(ri == ci, 1.0, 0.0).astype(F32)
    rc = lax.broadcasted_iota(jnp.int32, (c, c), 0)
    cc = lax.broadcasted_iota(jnp.int32, (c, c), 1)
    tri = jnp.where((rc - cc) * sgn >= 0, 1.0, 0.0).astype(BF16)
    sr = lax.broadcasted_iota(jnp.int32, (LANES, LANES), 0)
    sc = lax.broadcasted_iota(jnp.int32, (LANES, LANES), 1)
    bd_state = (sr // hd) == (sc // hd)
    lane = lax.broadcasted_iota(jnp.int32, (1, LANES), 1)
    m0 = jnp.where(lane < hd, 1.0, 0.0).astype(F32)
    m1 = 1.0 - m0
    blk = lambda k: (ri >> k) == (ci >> k)
    sib = [blk(1)] + [blk(k + 1) & jnp.logical_not(blk(k)) for k in range(1, int(math.log2(c)))]
    masks = (tri, bd_strict, bd_incl, bd_state, eye, m0, m1, sib)
    lanes = [slice(gi * LANES, (gi + 1) * LANES) for gi in range(n_grp)]
    cols = lambda ref: [ref[:, sl] for sl in lanes]
    y, new_state = _scan_chunk(cols(r_ref), cols(v_ref), cols(kk_ref), cols(lw_ref), cols(kd_ref), cols(bb_ref),
                               [state_scr[gi] for gi in range(n_grp)], masks)
    for gi in range(n_grp):
        y_ref[:, lanes[gi]] = y[gi]
        state_scr[gi] = new_state[gi]


def _rwkv_scan(r, v, kk, lw, kd, bb, geo):
    nt, d = r.shape
    c = SCAN_CHUNK
    b, s_len, l_len = geo["B"], geo["S"], geo["L"]
    n_lc, n_sc = l_len // c, s_len // c
    wl = SCAN_LANES if d % SCAN_LANES == 0 else LANES
    ctx_blk0 = (b * s_len) // c

    def row_blk(bi, dr, t):
        t_ctx = jnp.where(dr == 0, t, n_lc - 1 - t)
        t_lat = jnp.where(dr == 0, t - n_lc, n_sc - 1 - (t - n_lc))
        return jnp.where(t < n_lc, ctx_blk0 + bi * n_lc + t_ctx, bi * n_sc + t_lat)

    shared = pl.BlockSpec((c, wl), lambda bi, h, dr, t: (row_blk(bi, dr, t), h))
    per_dir = pl.BlockSpec((None, c, wl), lambda bi, h, dr, t: (dr, row_blk(bi, dr, t), h))
    return pl.pallas_call(
        _scan_kernel,
        grid=(b, d // wl, 2, n_lc + n_sc),
        in_specs=[shared, shared, shared, per_dir, per_dir, per_dir],
        out_specs=per_dir,
        out_shape=jax.ShapeDtypeStruct((2, nt, d), F32),
        scratch_shapes=[pltpu.VMEM((wl // LANES, LANES, LANES), F32)],
        compiler_params=_cparams(("parallel", "parallel", "parallel", "arbitrary")),
        name="rwkv_scan",
    )(r, v, kk, lw, kd, bb)


def _rwkv_post_kernel(ys_ref, r_ref, v_ref, kd_ref, g_ref, vec_ref, e_ref, o_ref):
    inv_n = 1.0 / RWKV_HEAD_DIM
    o = ys_ref[0] + ys_ref[1]
    mean = _head_sums(o, e_ref) * inv_n
    oc = o - mean
    var = _head_sums(oc * oc, e_ref) * inv_n
    o = oc * lax.rsqrt(var + LNX_EPS) * vec_ref[1:2, :] + vec_ref[2:3, :]
    bonus = _head_sums(r_ref[...] * (kd_ref[0] + kd_ref[1]) * vec_ref[0:1, :], e_ref)
    o = o + bonus * v_ref[...]
    o_ref[...] = (o * g_ref[...]).astype(BF16)


def _rwkv_post(ys, r, v, kd, g, p, geo, rows_out):
    nt, d = r.shape
    tm = geo["tm"]
    tn = _largest_divisor(d, (512, 256, 128))
    tok = pl.BlockSpec((tm, tn), lambda i, j: (i, j))
    tok2 = pl.BlockSpec((2, tm, tn), lambda i, j: (0, i, j))
    return pl.pallas_call(
        _rwkv_post_kernel,
        grid=(rows_out // tm, d // tn),
        in_specs=[tok2, tok, tok, tok2, tok,
                  pl.BlockSpec((8, tn), lambda i, j: (0, j)),
                  pl.BlockSpec((tn, tn), lambda i, j: (0, 0))],
        out_specs=tok,
        out_shape=jax.ShapeDtypeStruct((rows_out, d), BF16),
        compiler_params=_cparams(("parallel", "parallel")),
        name="rwkv_post",
    )(ys, r, v, kd, g, p["post_vecs"], p["head_ones"])


def _pad_rows(a, rows):
    return jnp.concatenate([a, jnp.zeros((rows - a.shape[0],) + a.shape[1:], a.dtype)], axis=0)


def _deinterleave_cols(w, n_heads):
    d = w.shape[0]
    return w.reshape(d, n_heads, ATTN_HEAD_DIM // 2, 2).swapaxes(2, 3).reshape(d, n_heads * ATTN_HEAD_DIM)


def _rope_tables(n_tokens, tm):
    rows = n_tokens // GRID_W
    row = jnp.repeat(jnp.arange(rows, dtype=F32), GRID_W)
    col = jnp.tile(jnp.arange(GRID_W, dtype=F32), rows)
    n_freq = ATTN_HEAD_DIM // 4
    inv = ROPE_THETA ** (-jnp.arange(n_freq, dtype=F32) / n_freq)
    ang = jnp.concatenate([row[:, None] * inv, col[:, None] * inv], axis=-1)
    cos, sin = jnp.cos(ang), jnp.sin(ang)
    cos_tab = jnp.concatenate([cos, cos], axis=-1)
    sin_tab = jnp.concatenate([-sin, sin], axis=-1)
    cos_tab = jnp.concatenate([cos_tab, jnp.ones((tm, ATTN_HEAD_DIM), F32)], axis=0)
    sin_tab = jnp.concatenate([sin_tab, jnp.zeros((tm, ATTN_HEAD_DIM), F32)], axis=0)
    return cos_tab, sin_tab


def kernel(x, c, ctx, c_ctx, ada_w, ada_b, norm_pre, norm_post, ffn_w_in, ffn_w_out, attn_w_qkv, attn_w_o, attn_q_gain, attn_k_gain, rwkv_mu, rwkv_w_rkv, rwkv_w_o, rwkv_g1, rwkv_g2, rwkv_dec_w0, rwkv_dec_w1, rwkv_dec_w2, rwkv_icl_a0, rwkv_icl_a1, rwkv_icl_a2, rwkv_k_k, rwkv_k_a, rwkv_r_k, rwkv_lnx_w, rwkv_lnx_b):
    b, s_len, d = x.shape
    l_len = ctx.shape[1]
    depth = ada_w.shape[0]
    n_lat_rows, n_ctx_rows = b * s_len, b * l_len
    nt = n_lat_rows + n_ctx_rows
    tm = _largest_divisor(math.gcd(s_len, n_ctx_rows), (512, 256, 128, 64, 32, 16, 8))
    tm_prep = min(tm, 256)
    tm_ffn = tm
    n_lat, tps = n_lat_rows // tm, s_len // tm

    def grp_for(tile):
        return lambda i: jnp.where(i < n_lat_rows // tile, 1 + i // (s_len // tile), 0)

    geo = dict(B=b, S=s_len, L=l_len, tm=tm, tm_prep=tm_prep, tm_ffn=tm_ffn, n_lat=n_lat, tps=tps,
               n_lat_rows=n_lat_rows, grp=grp_for(tm), grp_for=grp_for)
    assert s_len % SCAN_CHUNK == 0 and l_len % SCAN_CHUNK == 0 and n_lat_rows % l_len == 0
    n_q_heads = d // ATTN_HEAD_DIM
    n_kv_heads = n_q_heads // ATTN_GROUP
    q_cols, kv_cols = n_q_heads * ATTN_HEAD_DIM, n_kv_heads * ATTN_HEAD_DIM
    tq = min(256, l_len)

    stream = jnp.concatenate([x.reshape(n_lat_rows, d), ctx.reshape(n_ctx_rows, d)], axis=0)
    mod_rows = 8 * (-(-(1 + b) // 8))
    s_all = _pad_rows(jnp.concatenate([c_ctx[None, :], c], axis=0), mod_rows)
    mods = _ada_mods(s_all, ada_w, ada_b).reshape(depth, mod_rows, 9, d)

    ffn_w_in_b = ffn_w_in.astype(BF16)
    ffn_w_out_b = ffn_w_out.astype(BF16)
    cos_tab, sin_tab = _rope_tables(s_len, tm)
    tn_heads = _largest_divisor(d, (512, 256, 128))
    idx = jnp.arange(tn_heads) // RWKV_HEAD_DIM
    head_ones = (idx[:, None] == idx[None, :]).astype(BF16)

    for i in range(depth):
        last = i == depth - 1
        j = i // 2
        mod = lambda sub: mods[i, :, 3 * sub:3 * sub + 3]
        rows_after = n_lat_rows if last else nt
        stream = _ffn_half(stream, mod(0), norm_pre[i, 0], norm_post[i, 0], ffn_w_in_b, ffn_w_out_b, i, 0, geo, nt)
        if i % 2 == 0:
            w = attn_w_qkv[j]
            w_perm = jnp.concatenate([
                _deinterleave_cols(w[:, :q_cols], n_q_heads),
                _deinterleave_cols(w[:, q_cols:q_cols + kv_cols], n_kv_heads),
                w[:, q_cols + kv_cols:]], axis=1).astype(BF16)
            deint = lambda g: g.reshape(ATTN_HEAD_DIM // 2, 2).T.reshape(ATTN_HEAD_DIM)
            gains = jnp.stack([deint(attn_q_gain[j]), deint(attn_k_gain[j])])
            q, kt, v = _qkv_project(stream, mod(1), norm_pre[i, 1], w_perm, gains, cos_tab, sin_tab, geo,
                                    n_q_heads, n_kv_heads)
            o = _attention(q, kt, v, None, batch=b, n_kv_heads=n_kv_heads, q_row0=0, n_q=s_len,
                           srcs=[(n_lat_rows, l_len), (0, s_len)], tq=tq)
            if not last:
                o = _attention(q, kt, v, o, batch=b, n_kv_heads=n_kv_heads, q_row0=n_lat_rows, n_q=l_len,
                               srcs=[(n_lat_rows, l_len)], tq=tq)
            w_o = attn_w_o[j].astype(BF16)
        else:
            p = dict(
                mu8=_pad_rows(rwkv_mu[j], 8),
                w_rkv=rwkv_w_rkv[j].astype(BF16),
                dec_w1=rwkv_dec_w1[j].astype(BF16), dec_w2=rwkv_dec_w2[j].astype(BF16),
                icl_a1=rwkv_icl_a1[j].astype(BF16), icl_a2=rwkv_icl_a2[j].astype(BF16),
                g1=rwkv_g1[j].astype(BF16), g2=rwkv_g2[j].astype(BF16),
                vecs=_pad_rows(jnp.concatenate([rwkv_dec_w0[j], rwkv_icl_a0[j], rwkv_k_k[j][None], rwkv_k_a[j][None]],
                                               axis=0), 8),
                post_vecs=_pad_rows(jnp.stack([rwkv_r_k[j].reshape(d), rwkv_lnx_w[j], rwkv_lnx_b[j]]), 8),
                head_ones=head_ones,
            )
            y = _modnorm(stream, mod(1), norm_pre[i, 1], geo)
            r, v, kk, g, lw, kd, bb = _rwkv_prep(y, p, geo)
            ys = _rwkv_scan(r, v, kk, lw, kd, bb, geo)
            o = _rwkv_post(ys, r, v, kd, g, p, geo, rows_after)
            w_o = rwkv_w_o[j].astype(BF16)
        stream = _oproj_residual(o, stream, mod(1), norm_post[i, 1], w_o, geo, rows_after)
        stream = _ffn_half(stream, mod(2), norm_pre[i, 2], norm_post[i, 2], ffn_w_in_b, ffn_w_out_b, i, 1, geo,
                           rows_after)
    return stream.reshape(b, s_len, d)
```

```python
import functools
import math

import jax
import jax.numpy as jnp
from jax import lax
from jax.experimental import pallas as pl
from jax.experimental.pallas import tpu as pltpu

F32 = jnp.float32
BF16 = jnp.bfloat16

NORM_EPS = 1e-6
LNX_EPS = 64e-5
L2N_EPS = 1e-12
ATTN_HEAD_DIM = 128
ATTN_GROUP = 4
ROPE_THETA = 10000.0
GRID_W = 64
RWKV_HEAD_DIM = 64
LANES = 128
SCAN_CHUNK = 64
SCAN_LANES = 2048
VMEM_LIMIT = 56 * 1024 * 1024
DECAY_SCALE = math.exp(-0.5)


def _cparams(sem):
    return pltpu.CompilerParams(dimension_semantics=sem, vmem_limit_bytes=VMEM_LIMIT)


def _largest_divisor(n, candidates):
    for c in candidates:
        if n % c == 0:
            return c
    raise ValueError(f"no tile in {candidates} divides {n}")


def _dot(a, b):
    return jnp.dot(a, b, preferred_element_type=F32)


def _dot_nt(a, b):
    return lax.dot_general(a, b, (((1,), (1,)), ((), ())), preferred_element_type=F32)


def _split2(a):
    hi = a.astype(BF16)
    lo = (a - hi.astype(F32)).astype(BF16)
    return hi, lo


def _mm3(a, b, nt=False):
    d = _dot_nt if nt else _dot
    return d(a[0], b[0]) + (d(a[0], b[1]) + d(a[1], b[0]))


def _rms_rows(x, eps):
    return x * lax.rsqrt(jnp.mean(x * x, axis=-1, keepdims=True) + eps)


def _modulate(x, g_pre, shift, scale):
    return _rms_rows(x, NORM_EPS) * g_pre * (1.0 + scale) + shift


def _ada_kernel(s_ref, w_ref, b_ref, o_ref):
    s = s_ref[...]
    s = s * jax.nn.sigmoid(s)
    o_ref[...] = _mm3(_split2(s), _split2(w_ref[...])) + b_ref[...]


def _ada_mods(s_all, ada_w, ada_b):
    depth, d, n = ada_w.shape
    tn = _largest_divisor(n, (1024, 512, 256, 128))
    rows = s_all.shape[0]
    return pl.pallas_call(
        _ada_kernel,
        grid=(depth, n // tn),
        in_specs=[
            pl.BlockSpec((rows, d), lambda i, j: (0, 0)),
            pl.BlockSpec((None, d, tn), lambda i, j: (i, 0, j)),
            pl.BlockSpec((None, 1, tn), lambda i, j: (i, 0, j)),
        ],
        out_specs=pl.BlockSpec((None, rows, tn), lambda i, j: (i, 0, j)),
        out_shape=jax.ShapeDtypeStruct((depth, rows, n), F32),
        compiler_params=_cparams(("arbitrary", "arbitrary")),
        name="ada_mods",
    )(s_all, ada_w, ada_b.reshape(depth, 1, n))


def _ffn_kernel(x_ref, mod_ref, gpre_ref, gpost_ref, wg_ref, wu_ref, wo_ref, o_ref, y_scr, *, n_f):
    f = pl.program_id(1)

    def step(first, last):
        if first:
            y = _modulate(x_ref[...], gpre_ref[...], mod_ref[0:1, :], mod_ref[1:2, :]).astype(BF16)
            y_scr[...] = y
        else:
            y = y_scr[...]
        g = _dot(y, wg_ref[...])
        u = _dot(y, wu_ref[...])
        a = (g * jax.nn.sigmoid(g)) * u
        acc = _dot(a.astype(BF16), wo_ref[...])
        if not first:
            acc = o_ref[...] + acc
        if last:
            h = _rms_rows(acc, NORM_EPS) * gpost_ref[...]
            acc = x_ref[...] + 0.5 * (mod_ref[2:3, :] * h)
        o_ref[...] = acc

    if n_f == 1:
        step(True, True)
    else:
        pl.when(f == 0)(lambda: step(True, False))
        pl.when(f == n_f - 1)(lambda: step(False, True))
        if n_f > 2:
            pl.when(jnp.logical_and(f > 0, f < n_f - 1))(lambda: step(False, False))


def _ffn_half(stream, mod, g_pre, g_post, w_in, w_out, layer, half, geo, rows_out):
    nt, d = stream.shape
    f_dim = w_out.shape[2]
    tm = geo["tm_ffn"]
    tf = _largest_divisor(f_dim, (512, 256, 128))
    nf = f_dim // tf
    grp = geo["grp_for"](tm)
    return pl.pallas_call(
        functools.partial(_ffn_kernel, n_f=nf),
        grid=(rows_out // tm, nf),
        in_specs=[
            pl.BlockSpec((tm, d), lambda i, f: (i, 0)),
            pl.BlockSpec((None, 3, d), lambda i, f: (grp(i), 0, 0)),
            pl.BlockSpec((1, d), lambda i, f: (0, 0)),
            pl.BlockSpec((1, d), lambda i, f: (0, 0)),
            pl.BlockSpec((None, None, d, tf), lambda i, f: (layer, half, 0, f)),
            pl.BlockSpec((None, None, d, tf), lambda i, f: (layer, half, 0, nf + f)),
            pl.BlockSpec((None, None, tf, d), lambda i, f: (layer, half, f, 0)),
        ],
        out_specs=pl.BlockSpec((tm, d), lambda i, f: (i, 0)),
        out_shape=jax.ShapeDtypeStruct((rows_out, d), F32),
        scratch_shapes=[pltpu.VMEM((tm, d), BF16)],
        compiler_params=_cparams(("parallel", "arbitrary")),
        name="ffn_half",
    )(stream, mod, g_pre.reshape(1, d), g_post.reshape(1, d), w_in, w_in, w_out)


def _oproj_kernel(o_ref, x_ref, mod_ref, gpost_ref, w_ref, out_ref):
    h = _dot(o_ref[...], w_ref[...])
    h = _rms_rows(h, NORM_EPS) * gpost_ref[...]
    out_ref[...] = x_ref[...] + mod_ref[2:3, :] * h


def _oproj_residual(o, stream, mod, g_post, w_o, geo, rows_out):
    d = stream.shape[1]
    tm = geo["tm"]
    grp = geo["grp"]
    return pl.pallas_call(
        _oproj_kernel,
        grid=(rows_out // tm,),
        in_specs=[
            pl.BlockSpec((tm, d), lambda i: (i, 0)),
            pl.BlockSpec((tm, d), lambda i: (i, 0)),
            pl.BlockSpec((None, 3, d), lambda i: (grp(i), 0, 0)),
            pl.BlockSpec((1, d), lambda i: (0, 0)),
            pl.BlockSpec((d, d), lambda i: (0, 0)),
        ],
        out_specs=pl.BlockSpec((tm, d), lambda i: (i, 0)),
        out_shape=jax.ShapeDtypeStruct((rows_out, d), F32),
        compiler_params=_cparams(("parallel",)),
        name="oproj_residual",
    )(o, stream, mod, g_post.reshape(1, d), w_o)


def _qkv_kernel(x_ref, mod_ref, gpre_ref, w_ref, gains_ref, cos_ref, sin_ref, q_ref, kt_ref, v_ref,
                *, n_q_heads, n_kv_heads, q_scale):
    hd = ATTN_HEAD_DIM
    y = _modulate(x_ref[...], gpre_ref[...], mod_ref[0:1, :], mod_ref[1:2, :]).astype(BF16)
    cos = cos_ref[...]
    sin = sin_ref[...]

    def head_post(h, gain):
        hn = _rms_rows(h, NORM_EPS) * gain
        return hn * cos + pltpu.roll(hn, hd // 2, 1) * sin

    for c in range(n_q_heads // ATTN_GROUP):
        w = ATTN_GROUP * hd
        acc = _dot(y, w_ref[:, c * w:(c + 1) * w])
        for g in range(ATTN_GROUP):
            h = head_post(acc[:, g * hd:(g + 1) * hd], gains_ref[0:1, :]) * q_scale
            q_ref[:, (c * ATTN_GROUP + g) * hd:(c * ATTN_GROUP + g + 1) * hd] = h.astype(BF16)
    q_cols = n_q_heads * hd
    kv_cols = n_kv_heads * hd
    acc = _dot(y, w_ref[:, q_cols:q_cols + kv_cols])
    for h_i in range(n_kv_heads):
        kh = head_post(acc[:, h_i * hd:(h_i + 1) * hd], gains_ref[1:2, :])
        kt_ref[h_i * hd:(h_i + 1) * hd, :] = kh.T.astype(BF16)
    v_ref[...] = _dot(y, w_ref[:, q_cols + kv_cols:]).astype(BF16)


def _qkv_project(stream, mod, g_pre, w_qkv, gains, cos_tab, sin_tab, geo, n_q_heads, n_kv_heads):
    nt, d = stream.shape
    tm = geo["tm"]
    grp = geo["grp"]
    n_lat, tps = geo["n_lat"], geo["tps"]
    hd = ATTN_HEAD_DIM
    q_cols, kv_cols = n_q_heads * hd, n_kv_heads * hd
    rope_blk = lambda i: jnp.where(i < n_lat, i % tps, tps)
    return pl.pallas_call(
        functools.partial(_qkv_kernel, n_q_heads=n_q_heads, n_kv_heads=n_kv_heads,
                          q_scale=hd ** -0.5 * math.log2(math.e)),
        grid=(nt // tm,),
        in_specs=[
            pl.BlockSpec((tm, d), lambda i: (i, 0)),
            pl.BlockSpec((None, 3, d), lambda i: (grp(i), 0, 0)),
            pl.BlockSpec((1, d), lambda i: (0, 0)),
            pl.BlockSpec((d, q_cols + 2 * kv_cols), lambda i: (0, 0)),
            pl.BlockSpec((2, hd), lambda i: (0, 0)),
            pl.BlockSpec((tm, hd), lambda i: (rope_blk(i), 0)),
            pl.BlockSpec((tm, hd), lambda i: (rope_blk(i), 0)),
        ],
        out_specs=[
            pl.BlockSpec((tm, q_cols), lambda i: (i, 0)),
            pl.BlockSpec((kv_cols, tm), lambda i: (0, i)),
            pl.BlockSpec((tm, kv_cols), lambda i: (i, 0)),
        ],
        out_shape=[
            jax.ShapeDtypeStruct((nt, q_cols), BF16),
            jax.ShapeDtypeStruct((kv_cols, nt), BF16),
            jax.ShapeDtypeStruct((nt, kv_cols), BF16),
        ],
        compiler_params=_cparams(("parallel",)),
        name="qkv_project",
    )(stream, mod, g_pre.reshape(1, d), w_qkv, gains, cos_tab, sin_tab)


def _attn_kernel(*refs, src_lens, tk):
    q_ref = refs[0]
    o_ref = refs[-1]
    hd = ATTN_HEAD_DIM
    tq = q_ref.shape[0]
    q = q_ref[...]
    qs = jnp.concatenate([q[:, g * hd:(g + 1) * hd] for g in range(ATTN_GROUP)], axis=0)
    rows = ATTN_GROUP * tq
    m = jnp.full((rows, 1), -jnp.inf, F32)
    acc = jnp.zeros((rows, 2 * hd), F32)
    for s_i, n_keys in enumerate(src_lens):
        kt_ref, v_ref = refs[1 + 2 * s_i], refs[2 + 2 * s_i]
        step = min(tk, n_keys)
        ones = jnp.ones((step, hd), BF16)
        for c in range(n_keys // step):
            s = _dot(qs, kt_ref[:, c * step:(c + 1) * step])
            m_new = jnp.maximum(m, jnp.max(s, axis=-1, keepdims=True))
            p = jnp.exp2(s - m_new).astype(BF16)
            v_aug = jnp.concatenate([v_ref[c * step:(c + 1) * step, :], ones], axis=1)
            acc = jnp.exp2(m - m_new) * acc + _dot(p, v_aug)
            m = m_new
    o = acc[:, :hd] * (1.0 / acc[:, hd:])
    o_ref[...] = jnp.concatenate([o[g * tq:(g + 1) * tq] for g in range(ATTN_GROUP)], axis=1).astype(BF16)


def _attention(q, kt, v, o_prev, *, batch, n_kv_heads, q_row0, n_q, srcs, tq):
    hd = ATTN_HEAD_DIM
    gw = ATTN_GROUP * hd
    nt = q.shape[0]
    q_blk0 = q_row0 // tq
    in_specs = [pl.BlockSpec((tq, gw), lambda b, h, i: (q_blk0 + b * (n_q // tq) + i, h))]
    args = [q]
    for row0, n_keys in srcs:
        blk0 = row0 // n_keys
        in_specs.append(pl.BlockSpec((hd, n_keys), lambda b, h, i, blk0=blk0: (h, blk0 + b)))
        in_specs.append(pl.BlockSpec((n_keys, hd), lambda b, h, i, blk0=blk0: (blk0 + b, h)))
        args += [kt, v]
    aliases = {}
    if o_prev is not None:
        in_specs.append(pl.BlockSpec(memory_space=pl.ANY))
        args.append(o_prev)
        aliases = {len(args) - 1: 0}
    kern = functools.partial(_attn_kernel, src_lens=tuple(n for _, n in srcs), tk=256)
    if o_prev is not None:
        inner = kern
        kern = lambda *refs: inner(*refs[:-2], refs[-1])
    return pl.pallas_call(
        kern,
        grid=(batch, n_kv_heads, n_q // tq),
        in_specs=in_specs,
        out_specs=pl.BlockSpec((tq, gw), lambda b, h, i: (q_blk0 + b * (n_q // tq) + i, h)),
        out_shape=jax.ShapeDtypeStruct((nt, n_kv_heads * gw), BF16),
        input_output_aliases=aliases,
        compiler_params=_cparams(("parallel", "parallel", "arbitrary")),
        name="attention",
    )(*args)


def _head_sums(x, e_ref):
    hi, lo = _split2(x)
    e = e_ref[...]
    return _dot(hi, e) + _dot(lo, e)


def _rwkv_prep_kernel(x_ref, xp_ref, xn_ref, mod_ref, gpre_ref, mu_ref, w_ref, dw1_ref, ia1_ref, g1_ref, dw2_ref,
                      ia2_ref, g2_ref, vec_ref, e_ref,
                      r_ref, v_ref, kk_ref, g_ref, lw_ref, kd_ref, bb_ref,
                      xr_scr, xk_scr, xv_scr, hw_scr, ha_scr, hg_scr, *, n_lat_rows, seq_lat, seq_ctx):
    i = pl.program_id(0)
    j = pl.program_id(1)
    tm = x_ref.shape[0]

    def mixes():
        modulate = lambda x: _modulate(x, gpre_ref[...], mod_ref[0:1, :], mod_ref[1:2, :])
        y = modulate(x_ref[...])
        row = i * tm + lax.broadcasted_iota(jnp.int32, (tm, 1), 0)
        is_lat = row < n_lat_rows
        pos = jnp.where(is_lat, row % seq_lat, (row - n_lat_rows) % seq_ctx)
        seq_len = jnp.where(is_lat, seq_lat, seq_ctx)
        loc = lax.broadcasted_iota(jnp.int32, (tm, 1), 0)
        up = jnp.where(loc == 0, modulate(xp_ref[7:8, :]), pltpu.roll(y, 1, 0))
        up = jnp.where(pos == 0, 0.0, up)
        dn = jnp.where(loc == tm - 1, modulate(xn_ref[0:1, :]), pltpu.roll(y, tm - 1, 0))
        dn = jnp.where(pos == seq_len - 1, 0.0, dn)
        xx = 0.5 * (up + dn) - y
        mix = lambda m: (y + xx * mu_ref[m:m + 1, :]).astype(BF16)
        xr, xw, xk, xv, xa, xg = (mix(m) for m in range(6))
        hw = [jnp.tanh(_dot(xw, dw1_ref[dr])).astype(BF16) for dr in range(2)]
        ha = [_dot(xa, ia1_ref[dr]).astype(BF16) for dr in range(2)]
        hg = jax.nn.sigmoid(_dot(xg, g1_ref[...])).astype(BF16)
        xr_scr[...], xk_scr[...], xv_scr[...], hg_scr[...] = xr, xk, xv, hg
        for dr in range(2):
            hw_scr[dr], ha_scr[dr] = hw[dr], ha[dr]
        return xr, xk, xv, hw, ha, hg

    def column_tile(first):
        if first:
            xr, xk, xv, hw, ha, hg = mixes()
        else:
            xr, xk, xv, hg = xr_scr[...], xk_scr[...], xv_scr[...], hg_scr[...]
            hw, ha = [hw_scr[dr] for dr in range(2)], [ha_scr[dr] for dr in range(2)]
        r_ref[...] = _dot(xr, w_ref[0]).astype(BF16)
        k = _dot(xk, w_ref[1])
        v_ref[...] = _dot(xv, w_ref[2]).astype(BF16)
        g_ref[...] = _dot(hg, g2_ref[...]).astype(BF16)
        kx = k * vec_ref[4:5, :]
        kk = kx * lax.rsqrt(_head_sums(kx * kx, e_ref) + L2N_EPS)
        kk_ref[...] = kk.astype(BF16)
        k_a = vec_ref[5:6, :]
        for dr in range(2):
            w_pre = vec_ref[dr:dr + 1, :] + _dot(hw[dr], dw2_ref[dr])
            lw_ref[dr] = -DECAY_SCALE * jax.nn.sigmoid(w_pre)
            a = jax.nn.sigmoid(vec_ref[2 + dr:3 + dr, :] + _dot(ha[dr], ia2_ref[dr]))
            kd_ref[dr] = (k * (1.0 + (a - 1.0) * k_a)).astype(BF16)
            bb_ref[dr] = (kk * a).astype(BF16)

    pl.when(j == 0)(lambda: column_tile(True))
    pl.when(j != 0)(lambda: column_tile(False))


def _rwkv_prep(stream, mod, g_pre, p, geo):
    nt, d = stream.shape
    tm = geo["tm_prep"]
    grp = geo["grp_for"](tm)
    tn = _largest_divisor(d, (256, 128))
    lo_w = p["dec_w1"].shape[-1]
    lo_a = p["icl_a1"].shape[-1]
    lo_g = p["g1"].shape[-1]
    hb = tm // 8
    n_blk8 = nt // 8
    const3 = lambda i, j: (0, 0, 0)
    col3 = lambda i, j: (0, 0, j)
    tok = pl.BlockSpec((tm, tn), lambda i, j: (i, j))
    tok2 = pl.BlockSpec((2, tm, tn), lambda i, j: (0, i, j))
    kern = functools.partial(_rwkv_prep_kernel, n_lat_rows=geo["n_lat_rows"], seq_lat=geo["S"], seq_ctx=geo["L"])
    return pl.pallas_call(
        kern,
        grid=(nt // tm, d // tn),
        in_specs=[
            pl.BlockSpec((tm, d), lambda i, j: (i, 0)),
            pl.BlockSpec((8, d), lambda i, j: (jnp.maximum(i * hb - 1, 0), 0)),
            pl.BlockSpec((8, d), lambda i, j: (jnp.minimum((i + 1) * hb, n_blk8 - 1), 0)),
            pl.BlockSpec((None, 3, d), lambda i, j: (grp(i), 0, 0)),
            pl.BlockSpec((1, d), lambda i, j: (0, 0)),
            pl.BlockSpec((8, d), lambda i, j: (0, 0)),
            pl.BlockSpec((3, d, tn), col3),
            pl.BlockSpec((2, d, lo_w), const3),
            pl.BlockSpec((2, d, lo_a), const3),
            pl.BlockSpec((d, lo_g), lambda i, j: (0, 0)),
            pl.BlockSpec((2, lo_w, tn), col3),
            pl.BlockSpec((2, lo_a, tn), col3),
            pl.BlockSpec((lo_g, tn), lambda i, j: (0, j)),
            pl.BlockSpec((8, tn), lambda i, j: (0, j)),
            pl.BlockSpec((tn, tn), lambda i, j: (0, 0)),
        ],
        out_specs=[tok, tok, tok, tok, tok2, tok2, tok2],
        out_shape=[jax.ShapeDtypeStruct((nt, d), BF16)] * 4 + [
            jax.ShapeDtypeStruct((2, nt, d), F32), jax.ShapeDtypeStruct((2, nt, d), BF16),
            jax.ShapeDtypeStruct((2, nt, d), BF16)],
        scratch_shapes=[pltpu.VMEM((tm, d), BF16)] * 3 + [
            pltpu.VMEM((2, tm, lo_w), BF16), pltpu.VMEM((2, tm, lo_a), BF16), pltpu.VMEM((tm, lo_g), BF16)],
        compiler_params=_cparams(("parallel", "arbitrary")),
        name="rwkv_prep",
    )(stream, stream, stream, mod, g_pre.reshape(1, d), p["mu8"], p["w_rkv"], p["dec_w1"], p["icl_a1"], p["g1"],
      p["dec_w2"], p["icl_a2"], p["g2"], p["vecs"], p["head_ones"])


def _scan_chunk(r, v, kk, lw, kd, bb, state, masks):
    tri, bd_strict, bd_incl, bd_state, eye, m0, m1, sib = masks
    c = r[0].shape[0]
    c2 = 2 * c
    each = lambda fn, *cols: [fn(*args) for args in zip(*cols)]
    unstack = lambda a: a[:c] * m0 + a[c:] * m1

    def split3(x):
        hi = x.astype(BF16)
        res = x - hi.astype(F32)
        mid = res.astype(BF16)
        return jnp.concatenate([hi, mid, (res - mid.astype(F32)).astype(BF16)], axis=1)

    cum3 = each(lambda x: _dot(tri, split3(x)), lw)
    cum = each(lambda x: x[:, :LANES] + (x[:, LANES:2 * LANES] + x[:, 2 * LANES:]), cum3)
    tot = each(lambda x: jnp.sum(x, axis=0, keepdims=True), lw)
    e_neg = each(lambda x: jnp.exp(-x), cum)
    rt = each(lambda x, cm: x * jnp.exp(cm), r, cum)
    kkt = each(lambda x, cm, w: x * jnp.exp(cm - w), kk, cum, lw)
    lhs = each(lambda k_, r_: jnp.concatenate([(k_ * m0).astype(BF16), (k_ * m1).astype(BF16),
                                               (r_ * m0).astype(BF16), (r_ * m1).astype(BF16)], axis=0), kkt, rt)

    def make_rhs(kd_, bb_, en):
        kh = (kd_ * en).astype(BF16)
        bh = (bb_ * en).astype(BF16)
        return jnp.concatenate([kh, kh, bh, bh], axis=0)

    rhs = each(make_rhs, kd, bb, e_neg)
    a_all = each(_dot_nt, lhs, rhs)
    a_kkb = each(lambda a: jnp.where(bd_strict, a[:c2, c2:], 0.0), a_all)
    t_inv = each(lambda a: eye - jnp.where(sib[0], a, 0.0), a_kkb)
    for lvl in range(1, len(sib)):
        def merge(t, a, lvl=lvl):
            tb = t.astype(BF16)
            return t - _dot(_dot(tb, jnp.where(sib[lvl], a, 0.0).astype(BF16)).astype(BF16), tb)
        t_inv = each(merge, t_inv, a_kkb)
    st = each(lambda s: s.astype(BF16), state)
    v2 = each(lambda x: jnp.concatenate([x.astype(BF16)] * 2, axis=0), v)
    x2 = each(lambda l_, s_, a, v_: _dot_nt(l_[:c2], s_) + _dot(jnp.where(bd_strict, a[:c2, :c2], 0.0).astype(BF16), v_),
              lhs, st, a_all, v2)
    u2 = each(lambda t, x: _dot(t.astype(BF16), x.astype(BF16)), t_inv, x2)

    def outputs(l_, s_, a, v_, u_):
        a_rk = jnp.where(bd_incl, a[c2:, :c2], 0.0).astype(BF16)
        a_rb_neg = jnp.where(bd_incl, -a[c2:, c2:], 0.0).astype(BF16)
        y2 = _dot_nt(l_[c2:], s_) + _dot(jnp.concatenate([a_rk, a_rb_neg], axis=1),
                                         jnp.concatenate([v_, u_.astype(BF16)], axis=0))
        return unstack(y2)

    y = each(outputs, lhs, st, a_all, v2, u2)

    def state_update(s_, v_, u_, kd_, bb_, cm, tt):
        e_rem = jnp.exp(tt - cm)
        vu_t = jnp.concatenate([v_.astype(F32), -unstack(u_)], axis=0).T.astype(BF16)
        ds = _dot(vu_t, jnp.concatenate([(kd_ * e_rem).astype(BF16), (bb_ * e_rem).astype(BF16)], axis=0))
        return s_ * jnp.exp(tt) + jnp.where(bd_state, ds, 0.0)

    new_state = each(state_update, state, v, u2, kd, bb, cum, tot)
    return y, new_state


def _scan_kernel(r_ref, v_ref, kk_ref, lw_ref, kd_ref, bb_ref, y_ref, state_scr):
    dr = pl.program_id(2)
    t = pl.program_id(3)
    c = r_ref.shape[0]
    n_grp = r_ref.shape[1] // LANES
    hd = RWKV_HEAD_DIM

    @pl.when(t == 0)
    def _():
        state_scr[...] = jnp.zeros_like(state_scr)

    ri = lax.broadcasted_iota(jnp.int32, (2 * c, 2 * c), 0)
    ci = lax.broadcasted_iota(jnp.int32, (2 * c, 2 * c), 1)
    sgn = 1 - 2 * dr
    order = jnp.where((ri // c) == (ci // c), (ri - ci) * sgn, -1)
    bd_strict = order > 0
    bd_incl = order >= 0
    eye = jnp.where(ri == ci, 1.0, 0.0).astype(F32)
    rc = lax.broadcasted_iota(jnp.int32, (c, c), 0)
    cc = lax.broadcasted_iota(jnp.int32, (c, c), 1)
    tri = jnp.where((rc - cc) * sgn >= 0, 1.0, 0.0).astype(BF16)
    sr = lax.broadcasted_iota(jnp.int32, (LANES, LANES), 0)
    sc = lax.broadcasted_iota(jnp.int32, (LANES, LANES), 1)
    bd_state = (sr // hd) == (sc // hd)
    lane = lax.broadcasted_iota(jnp.int32, (1, LANES), 1)
    m0 = jnp.where(lane < hd, 1.0, 0.0).astype(F32)
    m1 = 1.0 - m0
    blk = lambda k: (ri >> k) == (ci >> k)
    sib = [blk(1)] + [blk(k + 1) & jnp.logical_not(blk(k)) for k in range(1, int(math.log2(c)))]
    masks = (tri, bd_strict, bd_incl, bd_state, eye, m0, m1, sib)
    lanes = [slice(gi * LANES, (gi + 1) * LANES) for gi in range(n_grp)]
    cols = lambda ref: [ref[:, sl] for sl in lanes]
    y, new_state = _scan_chunk(cols(r_ref), cols(v_ref), cols(kk_ref), cols(lw_ref), cols(kd_ref), cols(bb_ref),
                               [state_scr[gi] for gi in range(n_grp)], masks)
    for gi in range(n_grp):
        y_ref[:, lanes[gi]] = y[gi]
        state_scr[gi] = new_state[gi]


def _rwkv_scan(r, v, kk, lw, kd, bb, geo):
    nt, d = r.shape
    c = SCAN_CHUNK
    b, s_len, l_len = geo["B"], geo["S"], geo["L"]
    n_lc, n_sc = l_len // c, s_len // c
    wl = SCAN_LANES if d % SCAN_LANES == 0 else LANES
    ctx_blk0 = (b * s_len) // c

    def row_blk(bi, dr, t):
        t_ctx = jnp.where(dr == 0, t, n_lc - 1 - t)
        t_lat = jnp.where(dr == 0, t - n_lc, n_sc - 1 - (t - n_lc))
        return jnp.where(t < n_lc, ctx_blk0 + bi * n_lc + t_ctx, bi * n_sc + t_lat)

    shared = pl.BlockSpec((c, wl), lambda bi, h, dr, t: (row_blk(bi, dr, t), h))
    per_dir = pl.BlockSpec((None, c, wl), lambda bi, h, dr, t: (dr, row_blk(bi, dr, t), h))
    return pl.pallas_call(
        _scan_kernel,
        grid=(b, d // wl, 2, n_lc + n_sc),
        in_specs=[shared, shared, shared, per_dir, per_dir, per_dir],
        out_specs=per_dir,
        out_shape=jax.ShapeDtypeStruct((2, nt, d), F32),
        scratch_shapes=[pltpu.VMEM((wl // LANES, LANES, LANES), F32)],
        compiler_params=_cparams(("parallel", "parallel", "parallel", "arbitrary")),
        name="rwkv_scan",
    )(r, v, kk, lw, kd, bb)


def _rwkv_post_kernel(ys_ref, r_ref, v_ref, kd_ref, g_ref, vec_ref, e_ref, o_ref):
    inv_n = 1.0 / RWKV_HEAD_DIM
    o = ys_ref[0] + ys_ref[1]
    mean = _head_sums(o, e_ref) * inv_n
    oc = o - mean
    var = _head_sums(oc * oc, e_ref) * inv_n
    o = oc * lax.rsqrt(var + LNX_EPS) * vec_ref[1:2, :] + vec_ref[2:3, :]
    k_sum = kd_ref[0].astype(F32) + kd_ref[1].astype(F32)
    bonus = _head_sums(r_ref[...].astype(F32) * k_sum * vec_ref[0:1, :], e_ref)
    o = o + bonus * v_ref[...].astype(F32)
    o_ref[...] = (o * g_ref[...].astype(F32)).astype(BF16)


def _rwkv_post(ys, r, v, kd, g, p, geo, rows_out):
    nt, d = r.shape
    tm = geo["tm"]
    tn = _largest_divisor(d, (512, 256, 128))
    tok = pl.BlockSpec((tm, tn), lambda i, j: (i, j))
    tok2 = pl.BlockSpec((2, tm, tn), lambda i, j: (0, i, j))
    return pl.pallas_call(
        _rwkv_post_kernel,
        grid=(rows_out // tm, d // tn),
        in_specs=[tok2, tok, tok, tok2, tok,
                  pl.BlockSpec((8, tn), lambda i, j: (0, j)),
                  pl.BlockSpec((tn, tn), lambda i, j: (0, 0))],
        out_specs=tok,
        out_shape=jax.ShapeDtypeStruct((rows_out, d), BF16),
        compiler_params=_cparams(("parallel", "parallel")),
        name="rwkv_post",
    )(ys, r, v, kd, g, p["post_vecs"], p["head_ones"])


def _pad_rows(a, rows):
    return jnp.concatenate([a, jnp.zeros((rows - a.shape[0],) + a.shape[1:], a.dtype)], axis=0)


def _deinterleave_cols(w, n_heads):
    d = w.shape[0]
    return w.reshape(d, n_heads, ATTN_HEAD_DIM // 2, 2).swapaxes(2, 3).reshape(d, n_heads * ATTN_HEAD_DIM)


def _rope_tables(n_tokens, tm):
    rows = n_tokens // GRID_W
    row = jnp.repeat(jnp.arange(rows, dtype=F32), GRID_W)
    col = jnp.tile(jnp.arange(GRID_W, dtype=F32), rows)
    n_freq = ATTN_HEAD_DIM // 4
    inv = ROPE_THETA ** (-jnp.arange(n_freq, dtype=F32) / n_freq)
    ang = jnp.concatenate([row[:, None] * inv, col[:, None] * inv], axis=-1)
    cos, sin = jnp.cos(ang), jnp.sin(ang)
    cos_tab = jnp.concatenate([cos, cos], axis=-1)
    sin_tab = jnp.concatenate([-sin, sin], axis=-1)
    cos_tab = jnp.concatenate([cos_tab, jnp.ones((tm, ATTN_HEAD_DIM), F32)], axis=0)
    sin_tab = jnp.concatenate([sin_tab, jnp.zeros((tm, ATTN_HEAD_DIM), F32)], axis=0)
    return cos_tab, sin_tab


def kernel(x, c, ctx, c_ctx, ada_w, ada_b, norm_pre, norm_post, ffn_w_in, ffn_w_out, attn_w_qkv, attn_w_o, attn_q_gain, attn_k_gain, rwkv_mu, rwkv_w_rkv, rwkv_w_o, rwkv_g1, rwkv_g2, rwkv_dec_w0, rwkv_dec_w1, rwkv_dec_w2, rwkv_icl_a0, rwkv_icl_a1, rwkv_icl_a2, rwkv_k_k, rwkv_k_a, rwkv_r_k, rwkv_lnx_w, rwkv_lnx_b):
    b, s_len, d = x.shape
    l_len = ctx.shape[1]
    depth = ada_w.shape[0]
    n_lat_rows, n_ctx_rows = b * s_len, b * l_len
    nt = n_lat_rows + n_ctx_rows
    tm = _largest_divisor(math.gcd(s_len, n_ctx_rows), (512, 256, 128, 64, 32, 16, 8))
    tm_prep = tm
    tm_ffn = tm
    n_lat, tps = n_lat_rows // tm, s_len // tm

    def grp_for(tile):
        return lambda i: jnp.where(i < n_lat_rows // tile, 1 + i // (s_len // tile), 0)

    geo = dict(B=b, S=s_len, L=l_len, tm=tm, tm_prep=tm_prep, tm_ffn=tm_ffn, n_lat=n_lat, tps=tps,
               n_lat_rows=n_lat_rows, grp=grp_for(tm), grp_for=grp_for)
    assert s_len % SCAN_CHUNK == 0 and l_len % SCAN_CHUNK == 0 and n_lat_rows % l_len == 0
    n_q_heads = d // ATTN_HEAD_DIM
    n_kv_heads = n_q_heads // ATTN_GROUP
    q_cols, kv_cols = n_q_heads * ATTN_HEAD_DIM, n_kv_heads * ATTN_HEAD_DIM
    tq = min(256, l_len)

    stream = jnp.concatenate([x.reshape(n_lat_rows, d), ctx.reshape(n_ctx_rows, d)], axis=0)
    mod_rows = 8 * (-(-(1 + b) // 8))
    s_all = _pad_rows(jnp.concatenate([c_ctx[None, :], c], axis=0), mod_rows)
    mods = _ada_mods(s_all, ada_w, ada_b).reshape(depth, mod_rows, 9, d)

    ffn_w_in_b = ffn_w_in.astype(BF16)
    ffn_w_out_b = ffn_w_out.astype(BF16)
    cos_tab, sin_tab = _rope_tables(s_len, tm)
    tn_heads = _largest_divisor(d, (512, 256, 128))
    idx = jnp.arange(tn_heads) // RWKV_HEAD_DIM
    head_ones = (idx[:, None] == idx[None, :]).astype(BF16)

    for i in range(depth):
        last = i == depth - 1
        j = i // 2
        mod = lambda sub: mods[i, :, 3 * sub:3 * sub + 3]
        rows_after = n_lat_rows if last else nt
        stream = _ffn_half(stream, mod(0), norm_pre[i, 0], norm_post[i, 0], ffn_w_in_b, ffn_w_out_b, i, 0, geo, nt)
        if i % 2 == 0:
            w = attn_w_qkv[j]
            w_perm = jnp.concatenate([
                _deinterleave_cols(w[:, :q_cols], n_q_heads),
                _deinterleave_cols(w[:, q_cols:q_cols + kv_cols], n_kv_heads),
                w[:, q_cols + kv_cols:]], axis=1).astype(BF16)
            deint = lambda g: g.reshape(ATTN_HEAD_DIM // 2, 2).T.reshape(ATTN_HEAD_DIM)
            gains = jnp.stack([deint(attn_q_gain[j]), deint(attn_k_gain[j])])
            q, kt, v = _qkv_project(stream, mod(1), norm_pre[i, 1], w_perm, gains, cos_tab, sin_tab, geo,
                                    n_q_heads, n_kv_heads)
            o = _attention(q, kt, v, None, batch=b, n_kv_heads=n_kv_heads, q_row0=0, n_q=s_len,
                           srcs=[(n_lat_rows, l_len), (0, s_len)], tq=tq)
            if not last:
                o = _attention(q, kt, v, o, batch=b, n_kv_heads=n_kv_heads, q_row0=n_lat_rows, n_q=l_len,
                               srcs=[(n_lat_rows, l_len)], tq=tq)
            w_o = attn_w_o[j].astype(BF16)
        else:
            p = dict(
                mu8=_pad_rows(rwkv_mu[j], 8),
                w_rkv=rwkv_w_rkv[j].astype(BF16),
                dec_w1=rwkv_dec_w1[j].astype(BF16), dec_w2=rwkv_dec_w2[j].astype(BF16),
                icl_a1=rwkv_icl_a1[j].astype(BF16), icl_a2=rwkv_icl_a2[j].astype(BF16),
                g1=rwkv_g1[j].astype(BF16), g2=rwkv_g2[j].astype(BF16),
                vecs=_pad_rows(jnp.concatenate([rwkv_dec_w0[j], rwkv_icl_a0[j], rwkv_k_k[j][None], rwkv_k_a[j][None]],
                                               axis=0), 8),
                post_vecs=_pad_rows(jnp.stack([rwkv_r_k[j].reshape(d), rwkv_lnx_w[j], rwkv_lnx_b[j]]), 8),
                head_ones=head_ones,
            )
            r, v, kk, g, lw, kd, bb = _rwkv_prep(stream, mod(1), norm_pre[i, 1], p, geo)
            ys = _rwkv_scan(r, v, kk, lw, kd, bb, geo)
            o = _rwkv_post(ys, r, v, kd, g, p, geo, rows_after)
            w_o = rwkv_w_o[j].astype(BF16)
        stream = _oproj_residual(o, stream, mod(1), norm_post[i, 1], w_o, geo, rows_after)
        stream = _ffn_half(stream, mod(2), norm_pre[i, 2], norm_post[i, 2], ffn_w_in_b, ffn_w_out_b, i, 1, geo,
                           rows_after)
    return stream.reshape(b, s_len, d)
```

```python
import functools
import math

import jax
import jax.numpy as jnp
from jax import lax
from jax.experimental import pallas as pl
from jax.experimental.pallas import tpu as pltpu

F32 = jnp.float32
BF16 = jnp.bfloat16

NORM_EPS = 1e-6
LNX_EPS = 64e-5
L2N_EPS = 1e-12
ATTN_HEAD_DIM = 128
ATTN_GROUP = 4
ROPE_THETA = 10000.0
GRID_W = 64
RWKV_HEAD_DIM = 64
LANES = 128
SCAN_CHUNK = 64
SCAN_LANES = 2048
VMEM_LIMIT = 56 * 1024 * 1024
DECAY_SCALE = math.exp(-0.5)


def _cparams(sem):
    return pltpu.CompilerParams(dimension_semantics=sem, vmem_limit_bytes=VMEM_LIMIT)


def _largest_divisor(n, candidates):
    for c in candidates:
        if n % c == 0:
            return c
    raise ValueError(f"no tile in {candidates} divides {n}")


def _dot(a, b):
    return jnp.dot(a, b, preferred_element_type=F32)


def _dot_nt(a, b):
    return lax.dot_general(a, b, (((1,), (1,)), ((), ())), preferred_element_type=F32)


def _split2(a):
    hi = a.astype(BF16)
    lo = (a - hi.astype(F32)).astype(BF16)
    return hi, lo


def _mm3(a, b, nt=False):
    d = _dot_nt if nt else _dot
    return d(a[0], b[0]) + (d(a[0], b[1]) + d(a[1], b[0]))


def _rms_rows(x, eps):
    return x * lax.rsqrt(jnp.mean(x * x, axis=-1, keepdims=True) + eps)


def _modulate(x, g_pre, shift, scale):
    return _rms_rows(x, NORM_EPS) * g_pre * (1.0 + scale) + shift


def _ada_kernel(s_ref, w_ref, b_ref, o_ref):
    s = s_ref[...]
    s = s * jax.nn.sigmoid(s)
    o_ref[...] = _mm3(_split2(s), _split2(w_ref[...])) + b_ref[...]


def _ada_mods(s_all, ada_w, ada_b):
    depth, d, n = ada_w.shape
    tn = _largest_divisor(n, (1024, 512, 256, 128))
    rows = s_all.shape[0]
    return pl.pallas_call(
        _ada_kernel,
        grid=(depth, n // tn),
        in_specs=[
            pl.BlockSpec((rows, d), lambda i, j: (0, 0)),
            pl.BlockSpec((None, d, tn), lambda i, j: (i, 0, j)),
            pl.BlockSpec((None, 1, tn), lambda i, j: (i, 0, j)),
        ],
        out_specs=pl.BlockSpec((None, rows, tn), lambda i, j: (i, 0, j)),
        out_shape=jax.ShapeDtypeStruct((depth, rows, n), F32),
        compiler_params=_cparams(("arbitrary", "arbitrary")),
        name="ada_mods",
    )(s_all, ada_w, ada_b.reshape(depth, 1, n))


def _ffn_kernel(x_ref, mod_ref, gpre_ref, gpost_ref, wg_ref, wu_ref, wo_ref, o_ref, y_scr, *, n_f):
    f = pl.program_id(1)

    def step(first, last):
        if first:
            y = _modulate(x_ref[...], gpre_ref[...], mod_ref[0:1, :], mod_ref[1:2, :]).astype(BF16)
            y_scr[...] = y
        else:
            y = y_scr[...]
        g = _dot(y, wg_ref[...])
        u = _dot(y, wu_ref[...])
        a = (g * jax.nn.sigmoid(g)) * u
        acc = _dot(a.astype(BF16), wo_ref[...])
        if not first:
            acc = o_ref[...] + acc
        if last:
            h = _rms_rows(acc, NORM_EPS) * gpost_ref[...]
            acc = x_ref[...] + 0.5 * (mod_ref[2:3, :] * h)
        o_ref[...] = acc

    if n_f == 1:
        step(True, True)
    else:
        pl.when(f == 0)(lambda: step(True, False))
        pl.when(f == n_f - 1)(lambda: step(False, True))
        if n_f > 2:
            pl.when(jnp.logical_and(f > 0, f < n_f - 1))(lambda: step(False, False))


def _ffn_half(stream, mod, g_pre, g_post, w_in, w_out, layer, half, geo, rows_out):
    nt, d = stream.shape
    f_dim = w_out.shape[2]
    tm = geo["tm_ffn"]
    tf = _largest_divisor(f_dim, (512, 256, 128))
    nf = f_dim // tf
    grp = geo["grp_for"](tm)
    return pl.pallas_call(
        functools.partial(_ffn_kernel, n_f=nf),
        grid=(rows_out // tm, nf),
        in_specs=[
            pl.BlockSpec((tm, d), lambda i, f: (i, 0)),
            pl.BlockSpec((None, 3, d), lambda i, f: (grp(i), 0, 0)),
            pl.BlockSpec((1, d), lambda i, f: (0, 0)),
            pl.BlockSpec((1, d), lambda i, f: (0, 0)),
            pl.BlockSpec((None, None, d, tf), lambda i, f: (layer, half, 0, f)),
            pl.BlockSpec((None, None, d, tf), lambda i, f: (layer, half, 0, nf + f)),
            pl.BlockSpec((None, None, tf, d), lambda i, f: (layer, half, f, 0)),
        ],
        out_specs=pl.BlockSpec((tm, d), lambda i, f: (i, 0)),
        out_shape=jax.ShapeDtypeStruct((rows_out, d), F32),
        scratch_shapes=[pltpu.VMEM((tm, d), BF16)],
        compiler_params=_cparams(("parallel", "arbitrary")),
        name="ffn_half",
    )(stream, mod, g_pre.reshape(1, d), g_post.reshape(1, d), w_in, w_in, w_out)


def _oproj_kernel(o_ref, x_ref, mod_ref, gpost_ref, w_ref, out_ref):
    h = _dot(o_ref[...], w_ref[...])
    h = _rms_rows(h, NORM_EPS) * gpost_ref[...]
    out_ref[...] = x_ref[...] + mod_ref[2:3, :] * h


def _oproj_residual(o, stream, mod, g_post, w_o, geo, rows_out):
    d = stream.shape[1]
    tm = geo["tm"]
    grp = geo["grp"]
    return pl.pallas_call(
        _oproj_kernel,
        grid=(rows_out // tm,),
        in_specs=[
            pl.BlockSpec((tm, d), lambda i: (i, 0)),
            pl.BlockSpec((tm, d), lambda i: (i, 0)),
            pl.BlockSpec((None, 3, d), lambda i: (grp(i), 0, 0)),
            pl.BlockSpec((1, d), lambda i: (0, 0)),
            pl.BlockSpec((d, d), lambda i: (0, 0)),
        ],
        out_specs=pl.BlockSpec((tm, d), lambda i: (i, 0)),
        out_shape=jax.ShapeDtypeStruct((rows_out, d), F32),
        compiler_params=_cparams(("parallel",)),
        name="oproj_residual",
    )(o, stream, mod, g_post.reshape(1, d), w_o)


def _qkv_kernel(x_ref, mod_ref, gpre_ref, w_ref, gains_ref, cos_ref, sin_ref, q_ref, kt_ref, v_ref,
                *, n_q_heads, n_kv_heads, q_scale):
    hd = ATTN_HEAD_DIM
    y = _modulate(x_ref[...], gpre_ref[...], mod_ref[0:1, :], mod_ref[1:2, :]).astype(BF16)
    cos = cos_ref[...]
    sin = sin_ref[...]

    def head_post(h, gain):
        hn = _rms_rows(h, NORM_EPS) * gain
        return hn * cos + pltpu.roll(hn, hd // 2, 1) * sin

    for c in range(n_q_heads // ATTN_GROUP):
        w = ATTN_GROUP * hd
        acc = _dot(y, w_ref[:, c * w:(c + 1) * w])
        for g in range(ATTN_GROUP):
            h = head_post(acc[:, g * hd:(g + 1) * hd], gains_ref[0:1, :]) * q_scale
            q_ref[:, (c * ATTN_GROUP + g) * hd:(c * ATTN_GROUP + g + 1) * hd] = h.astype(BF16)
    q_cols = n_q_heads * hd
    kv_cols = n_kv_heads * hd
    acc = _dot(y, w_ref[:, q_cols:q_cols + kv_cols])
    for h_i in range(n_kv_heads):
        kh = head_post(acc[:, h_i * hd:(h_i + 1) * hd], gains_ref[1:2, :])
        kt_ref[h_i * hd:(h_i + 1) * hd, :] = kh.T.astype(BF16)
    v_ref[...] = _dot(y, w_ref[:, q_cols + kv_cols:]).astype(BF16)


def _qkv_project(stream, mod, g_pre, w_qkv, gains, cos_tab, sin_tab, geo, n_q_heads, n_kv_heads):
    nt, d = stream.shape
    tm = geo["tm"]
    grp = geo["grp"]
    n_lat, tps = geo["n_lat"], geo["tps"]
    hd = ATTN_HEAD_DIM
    q_cols, kv_cols = n_q_heads * hd, n_kv_heads * hd
    rope_blk = lambda i: jnp.where(i < n_lat, i % tps, tps)
    return pl.pallas_call(
        functools.partial(_qkv_kernel, n_q_heads=n_q_heads, n_kv_heads=n_kv_heads,
                          q_scale=hd ** -0.5 * math.log2(math.e)),
        grid=(nt // tm,),
        in_specs=[
            pl.BlockSpec((tm, d), lambda i: (i, 0)),
            pl.BlockSpec((None, 3, d), lambda i: (grp(i), 0, 0)),
            pl.BlockSpec((1, d), lambda i: (0, 0)),
            pl.BlockSpec((d, q_cols + 2 * kv_cols), lambda i: (0, 0)),
            pl.BlockSpec((2, hd), lambda i: (0, 0)),
            pl.BlockSpec((tm, hd), lambda i: (rope_blk(i), 0)),
            pl.BlockSpec((tm, hd), lambda i: (rope_blk(i), 0)),
        ],
        out_specs=[
            pl.BlockSpec((tm, q_cols), lambda i: (i, 0)),
            pl.BlockSpec((kv_cols, tm), lambda i: (0, i)),
            pl.BlockSpec((tm, kv_cols), lambda i: (i, 0)),
        ],
        out_shape=[
            jax.ShapeDtypeStruct((nt, q_cols), BF16),
            jax.ShapeDtypeStruct((kv_cols, nt), BF16),
            jax.ShapeDtypeStruct((nt, kv_cols), BF16),
        ],
        compiler_params=_cparams(("parallel",)),
        name="qkv_project",
    )(stream, mod, g_pre.reshape(1, d), w_qkv, gains, cos_tab, sin_tab)


def _attn_kernel(*refs, src_lens, tk):
    q_ref = refs[0]
    o_ref = refs[-1]
    hd = ATTN_HEAD_DIM
    tq = q_ref.shape[0]
    q = q_ref[...]
    qs = jnp.concatenate([q[:, g * hd:(g + 1) * hd] for g in range(ATTN_GROUP)], axis=0)
    rows = ATTN_GROUP * tq
    m = jnp.full((rows, 1), -jnp.inf, F32)
    acc = jnp.zeros((rows, 2 * hd), F32)
    for s_i, n_keys in enumerate(src_lens):
        kt_ref, v_ref = refs[1 + 2 * s_i], refs[2 + 2 * s_i]
        step = min(tk, n_keys)
        ones = jnp.ones((step, hd), BF16)
        for c in range(n_keys // step):
            s = _dot(qs, kt_ref[:, c * step:(c + 1) * step])
            m_new = jnp.maximum(m, jnp.max(s, axis=-1, keepdims=True))
            p = jnp.exp2(s - m_new).astype(BF16)
            v_aug = jnp.concatenate([v_ref[c * step:(c + 1) * step, :], ones], axis=1)
            acc = jnp.exp2(m - m_new) * acc + _dot(p, v_aug)
            m = m_new
    o = acc[:, :hd] * (1.0 / acc[:, hd:])
    o_ref[...] = jnp.concatenate([o[g * tq:(g + 1) * tq] for g in range(ATTN_GROUP)], axis=1).astype(BF16)


def _attention(q, kt, v, o_prev, *, batch, n_kv_heads, q_row0, n_q, srcs, tq):
    hd = ATTN_HEAD_DIM
    gw = ATTN_GROUP * hd
    nt = q.shape[0]
    q_blk0 = q_row0 // tq
    in_specs = [pl.BlockSpec((tq, gw), lambda b, h, i: (q_blk0 + b * (n_q // tq) + i, h))]
    args = [q]
    for row0, n_keys in srcs:
        blk0 = row0 // n_keys
        in_specs.append(pl.BlockSpec((hd, n_keys), lambda b, h, i, blk0=blk0: (h, blk0 + b)))
        in_specs.append(pl.BlockSpec((n_keys, hd), lambda b, h, i, blk0=blk0: (blk0 + b, h)))
        args += [kt, v]
    aliases = {}
    if o_prev is not None:
        in_specs.append(pl.BlockSpec(memory_space=pl.ANY))
        args.append(o_prev)
        aliases = {len(args) - 1: 0}
    kern = functools.partial(_attn_kernel, src_lens=tuple(n for _, n in srcs), tk=256)
    if o_prev is not None:
        inner = kern
        kern = lambda *refs: inner(*refs[:-2], refs[-1])
    return pl.pallas_call(
        kern,
        grid=(batch, n_kv_heads, n_q // tq),
        in_specs=in_specs,
        out_specs=pl.BlockSpec((tq, gw), lambda b, h, i: (q_blk0 + b * (n_q // tq) + i, h)),
        out_shape=jax.ShapeDtypeStruct((nt, n_kv_heads * gw), BF16),
        input_output_aliases=aliases,
        compiler_params=_cparams(("parallel", "parallel", "arbitrary")),
        name="attention",
    )(*args)


def _head_sums(x, e_ref):
    hi, lo = _split2(x)
    e = e_ref[...]
    return _dot(hi, e) + _dot(lo, e)


def _rwkv_prep_kernel(x_ref, xp_ref, xn_ref, mod_ref, gpre_ref, mu_ref, w_ref, dw1_ref, ia1_ref, g1_ref, dw2_ref,
                      ia2_ref, g2_ref, vec_ref, e_ref,
                      r_ref, v_ref, kk_ref, g_ref, lw_ref, kd_ref, bb_ref,
                      xr_scr, xk_scr, xv_scr, hw_scr, ha_scr, hg_scr, *, n_lat_rows, seq_lat, seq_ctx):
    i = pl.program_id(0)
    j = pl.program_id(1)
    tm = x_ref.shape[0]

    def mixes():
        modulate = lambda x: _modulate(x, gpre_ref[...], mod_ref[0:1, :], mod_ref[1:2, :])
        y = modulate(x_ref[...])
        row = i * tm + lax.broadcasted_iota(jnp.int32, (tm, 1), 0)
        is_lat = row < n_lat_rows
        pos = jnp.where(is_lat, row % seq_lat, (row - n_lat_rows) % seq_ctx)
        seq_len = jnp.where(is_lat, seq_lat, seq_ctx)
        loc = lax.broadcasted_iota(jnp.int32, (tm, 1), 0)
        up = jnp.where(loc == 0, modulate(xp_ref[7:8, :]), pltpu.roll(y, 1, 0))
        up = jnp.where(pos == 0, 0.0, up)
        dn = jnp.where(loc == tm - 1, modulate(xn_ref[0:1, :]), pltpu.roll(y, tm - 1, 0))
        dn = jnp.where(pos == seq_len - 1, 0.0, dn)
        xx = 0.5 * (up + dn) - y
        mix = lambda m: (y + xx * mu_ref[m:m + 1, :]).astype(BF16)
        xr, xw, xk, xv, xa, xg = (mix(m) for m in range(6))
        hw = [jnp.tanh(_dot(xw, dw1_ref[dr])).astype(BF16) for dr in range(2)]
        ha = [_dot(xa, ia1_ref[dr]).astype(BF16) for dr in range(2)]
        hg = jax.nn.sigmoid(_dot(xg, g1_ref[...])).astype(BF16)
        xr_scr[...], xk_scr[...], xv_scr[...], hg_scr[...] = xr, xk, xv, hg
        for dr in range(2):
            hw_scr[dr], ha_scr[dr] = hw[dr], ha[dr]
        return xr, xk, xv, hw, ha, hg

    def column_tile(first):
        if first:
            xr, xk, xv, hw, ha, hg = mixes()
        else:
            xr, xk, xv, hg = xr_scr[...], xk_scr[...], xv_scr[...], hg_scr[...]
            hw, ha = [hw_scr[dr] for dr in range(2)], [ha_scr[dr] for dr in range(2)]
        r_ref[...] = _dot(xr, w_ref[0]).astype(BF16)
        k = _dot(xk, w_ref[1])
        v_ref[...] = _dot(xv, w_ref[2]).astype(BF16)
        g_ref[...] = _dot(hg, g2_ref[...]).astype(BF16)
        kx = k * vec_ref[4:5, :]
        kk = kx * lax.rsqrt(_head_sums(kx * kx, e_ref) + L2N_EPS)
        kk_ref[...] = kk.astype(BF16)
        k_a = vec_ref[5:6, :]
        for dr in range(2):
            w_pre = vec_ref[dr:dr + 1, :] + _dot(hw[dr], dw2_ref[dr])
            lw_ref[dr] = -DECAY_SCALE * jax.nn.sigmoid(w_pre)
            a = jax.nn.sigmoid(vec_ref[2 + dr:3 + dr, :] + _dot(ha[dr], ia2_ref[dr]))
            kd_ref[dr] = (k * (1.0 + (a - 1.0) * k_a)).astype(BF16)
            bb_ref[dr] = (kk * a).astype(BF16)

    pl.when(j == 0)(lambda: column_tile(True))
    pl.when(j != 0)(lambda: column_tile(False))


def _rwkv_prep(stream, mod, g_pre, p, geo):
    nt, d = stream.shape
    tm = geo["tm_prep"]
    grp = geo["grp_for"](tm)
    tn = _largest_divisor(d, (256, 128))
    lo_w = p["dec_w1"].shape[-1]
    lo_a = p["icl_a1"].shape[-1]
    lo_g = p["g1"].shape[-1]
    hb = tm // 8
    n_blk8 = nt // 8
    const3 = lambda i, j: (0, 0, 0)
    col3 = lambda i, j: (0, 0, j)
    tok = pl.BlockSpec((tm, tn), lambda i, j: (i, j))
    tok2 = pl.BlockSpec((2, tm, tn), lambda i, j: (0, i, j))
    kern = functools.partial(_rwkv_prep_kernel, n_lat_rows=geo["n_lat_rows"], seq_lat=geo["S"], seq_ctx=geo["L"])
    return pl.pallas_call(
        kern,
        grid=(nt // tm, d // tn),
        in_specs=[
            pl.BlockSpec((tm, d), lambda i, j: (i, 0)),
            pl.BlockSpec((8, d), lambda i, j: (jnp.maximum(i * hb - 1, 0), 0)),
            pl.BlockSpec((8, d), lambda i, j: (jnp.minimum((i + 1) * hb, n_blk8 - 1), 0)),
            pl.BlockSpec((None, 3, d), lambda i, j: (grp(i), 0, 0)),
            pl.BlockSpec((1, d), lambda i, j: (0, 0)),
            pl.BlockSpec((8, d), lambda i, j: (0, 0)),
            pl.BlockSpec((3, d, tn), col3),
            pl.BlockSpec((2, d, lo_w), const3),
            pl.BlockSpec((2, d, lo_a), const3),
            pl.BlockSpec((d, lo_g), lambda i, j: (0, 0)),
            pl.BlockSpec((2, lo_w, tn), col3),
            pl.BlockSpec((2, lo_a, tn), col3),
            pl.BlockSpec((lo_g, tn), lambda i, j: (0, j)),
            pl.BlockSpec((8, tn), lambda i, j: (0, j)),
            pl.BlockSpec((tn, tn), lambda i, j: (0, 0)),
        ],
        out_specs=[tok, tok, tok, tok, tok2, tok2, tok2],
        out_shape=[jax.ShapeDtypeStruct((nt, d), BF16)] * 4 + [
            jax.ShapeDtypeStruct((2, nt, d), F32), jax.ShapeDtypeStruct((2, nt, d), BF16),
            jax.ShapeDtypeStruct((2, nt, d), BF16)],
        scratch_shapes=[pltpu.VMEM((tm, d), BF16)] * 3 + [
            pltpu.VMEM((2, tm, lo_w), BF16), pltpu.VMEM((2, tm, lo_a), BF16), pltpu.VMEM((tm, lo_g), BF16)],
        compiler_params=_cparams(("parallel", "arbitrary")),
        name="rwkv_prep",
    )(stream, stream, stream, mod, g_pre.reshape(1, d), p["mu8"], p["w_rkv"], p["dec_w1"], p["icl_a1"], p["g1"],
      p["dec_w2"], p["icl_a2"], p["g2"], p["vecs"], p["head_ones"])


def _scan_chunk(r, v, kk, lw, kd, bb, state, order, masks):
    bd_state, eye, m0, m1, sib = masks
    tri, bd_strict, bd_incl = zip(*order)
    c = r[0].shape[0]
    c2 = 2 * c
    each = lambda fn, *cols: [fn(*args) for args in zip(*cols)]
    unstack = lambda a: a[:c] * m0 + a[c:] * m1

    def split3(x):
        hi = x.astype(BF16)
        res = x - hi.astype(F32)
        mid = res.astype(BF16)
        return jnp.concatenate([hi, mid, (res - mid.astype(F32)).astype(BF16)], axis=1)

    cum3 = each(lambda x, tr: _dot(tr, split3(x)), lw, tri)
    cum = each(lambda x: x[:, :LANES] + (x[:, LANES:2 * LANES] + x[:, 2 * LANES:]), cum3)
    tot = each(lambda x: jnp.sum(x, axis=0, keepdims=True), lw)
    e_neg = each(lambda x: jnp.exp(-x), cum)
    rt = each(lambda x, cm: x * jnp.exp(cm), r, cum)
    kkt = each(lambda x, cm, w: x * jnp.exp(cm - w), kk, cum, lw)
    lhs = each(lambda k_, r_: jnp.concatenate([(k_ * m0).astype(BF16), (k_ * m1).astype(BF16),
                                               (r_ * m0).astype(BF16), (r_ * m1).astype(BF16)], axis=0), kkt, rt)

    def make_rhs(kd_, bb_, en):
        kh = (kd_ * en).astype(BF16)
        bh = (bb_ * en).astype(BF16)
        return jnp.concatenate([kh, kh, bh, bh], axis=0)

    rhs = each(make_rhs, kd, bb, e_neg)
    a_all = each(_dot_nt, lhs, rhs)
    a_kkb = each(lambda a, strict: jnp.where(strict, a[:c2, c2:], 0.0), a_all, bd_strict)
    t_inv = each(lambda a: eye - jnp.where(sib[0], a, 0.0), a_kkb)
    for lvl in range(1, len(sib)):
        def merge(t, a, lvl=lvl):
            tb = t.astype(BF16)
            return t - _dot(_dot(tb, jnp.where(sib[lvl], a, 0.0).astype(BF16)).astype(BF16), tb)
        t_inv = each(merge, t_inv, a_kkb)
    st = each(lambda s: s.astype(BF16), state)
    v2 = each(lambda x: jnp.concatenate([x.astype(BF16)] * 2, axis=0), v)
    x2 = each(lambda l_, s_, a, v_, strict: _dot_nt(l_[:c2], s_) + _dot(jnp.where(strict, a[:c2, :c2], 0.0).astype(BF16), v_),
              lhs, st, a_all, v2, bd_strict)
    u2 = each(lambda t, x: _dot(t.astype(BF16), x.astype(BF16)), t_inv, x2)

    def outputs(l_, s_, a, v_, u_, incl):
        a_rk = jnp.where(incl, a[c2:, :c2], 0.0).astype(BF16)
        a_rb_neg = jnp.where(incl, -a[c2:, c2:], 0.0).astype(BF16)
        y2 = _dot_nt(l_[c2:], s_) + _dot(jnp.concatenate([a_rk, a_rb_neg], axis=1),
                                         jnp.concatenate([v_, u_.astype(BF16)], axis=0))
        return unstack(y2)

    y = each(outputs, lhs, st, a_all, v2, u2, bd_incl)

    def state_update(s_, v_, u_, kd_, bb_, cm, tt):
        e_rem = jnp.exp(tt - cm)
        vu_t = jnp.concatenate([v_.astype(F32), -unstack(u_)], axis=0).T.astype(BF16)
        ds = _dot(vu_t, jnp.concatenate([(kd_ * e_rem).astype(BF16), (bb_ * e_rem).astype(BF16)], axis=0))
        return s_ * jnp.exp(tt) + jnp.where(bd_state, ds, 0.0)

    new_state = each(state_update, state, v, u2, kd, bb, cum, tot)
    return y, new_state


def _scan_kernel(*refs):
    fwd, bwd = refs[0:6], refs[6:12]
    y_refs = refs[12:14]
    state_scr = refs[14]
    t = pl.program_id(2)
    c = fwd[0].shape[0]
    n_grp = fwd[0].shape[1] // LANES
    hd = RWKV_HEAD_DIM

    @pl.when(t == 0)
    def _():
        state_scr[...] = jnp.zeros_like(state_scr)

    ri = lax.broadcasted_iota(jnp.int32, (2 * c, 2 * c), 0)
    ci = lax.broadcasted_iota(jnp.int32, (2 * c, 2 * c), 1)
    rc = lax.broadcasted_iota(jnp.int32, (c, c), 0)
    cc = lax.broadcasted_iota(jnp.int32, (c, c), 1)
    same_head = (ri // c) == (ci // c)

    def order_masks(sgn):
        order = jnp.where(same_head, (ri - ci) * sgn, -1)
        tri = jnp.where((rc - cc) * sgn >= 0, 1.0, 0.0).astype(BF16)
        return tri, order > 0, order >= 0

    eye = jnp.where(ri == ci, 1.0, 0.0).astype(F32)
    sr = lax.broadcasted_iota(jnp.int32, (LANES, LANES), 0)
    sc = lax.broadcasted_iota(jnp.int32, (LANES, LANES), 1)
    bd_state = (sr // hd) == (sc // hd)
    lane = lax.broadcasted_iota(jnp.int32, (1, LANES), 1)
    m0 = jnp.where(lane < hd, 1.0, 0.0).astype(F32)
    m1 = 1.0 - m0
    blk = lambda k: (ri >> k) == (ci >> k)
    sib = [blk(1)] + [blk(k + 1) & jnp.logical_not(blk(k)) for k in range(1, int(math.log2(c)))]
    lanes = [slice(gi * LANES, (gi + 1) * LANES) for gi in range(n_grp)]
    cols = lambda i: [ref[:, sl] for ref in (fwd[i], bwd[i]) for sl in lanes]
    order = [order_masks(1)] * n_grp + [order_masks(-1)] * n_grp
    y, new_state = _scan_chunk(*[cols(i) for i in range(6)], [state_scr[gi] for gi in range(2 * n_grp)], order,
                               (bd_state, eye, m0, m1, sib))
    for gi in range(2 * n_grp):
        y_refs[gi // n_grp][:, lanes[gi % n_grp]] = y[gi].astype(BF16)
        state_scr[gi] = new_state[gi]


def _rwkv_scan(r, v, kk, lw, kd, bb, geo):
    nt, d = r.shape
    c = SCAN_CHUNK
    b, s_len, l_len = geo["B"], geo["S"], geo["L"]
    n_lc, n_sc = l_len // c, s_len // c
    wl = SCAN_LANES if d % SCAN_LANES == 0 else LANES
    ctx_blk0 = (b * s_len) // c

    def row_blk(bi, t, reverse):
        t_ctx = n_lc - 1 - t if reverse else t
        t_lat = n_sc - 1 - (t - n_lc) if reverse else t - n_lc
        return jnp.where(t < n_lc, ctx_blk0 + bi * n_lc + t_ctx, bi * n_sc + t_lat)

    def specs(reverse):
        shared = pl.BlockSpec((c, wl), lambda bi, h, t: (row_blk(bi, t, reverse), h))
        per_dir = pl.BlockSpec((None, c, wl), lambda bi, h, t: (int(reverse), row_blk(bi, t, reverse), h))
        return [shared, shared, shared, per_dir, per_dir, per_dir]

    return pl.pallas_call(
        _scan_kernel,
        grid=(b, d // wl, n_lc + n_sc),
        in_specs=specs(False) + specs(True),
        out_specs=[specs(False)[0], specs(True)[0]],
        out_shape=[jax.ShapeDtypeStruct((nt, d), BF16)] * 2,
        scratch_shapes=[pltpu.VMEM((2 * (wl // LANES), LANES, LANES), F32)],
        compiler_params=_cparams(("parallel", "parallel", "arbitrary")),
        name="rwkv_scan",
    )(r, v, kk, lw, kd, bb, r, v, kk, lw, kd, bb)


def _rwkv_post_kernel(yf_ref, yb_ref, r_ref, v_ref, kd_ref, g_ref, vec_ref, e_ref, o_ref):
    inv_n = 1.0 / RWKV_HEAD_DIM
    o = yf_ref[...].astype(F32) + yb_ref[...].astype(F32)
    mean = _head_sums(o, e_ref) * inv_n
    oc = o - mean
    var = _head_sums(oc * oc, e_ref) * inv_n
    o = oc * lax.rsqrt(var + LNX_EPS) * vec_ref[1:2, :] + vec_ref[2:3, :]
    k_sum = kd_ref[0].astype(F32) + kd_ref[1].astype(F32)
    bonus = _head_sums(r_ref[...].astype(F32) * k_sum * vec_ref[0:1, :], e_ref)
    o = o + bonus * v_ref[...].astype(F32)
    o_ref[...] = (o * g_ref[...].astype(F32)).astype(BF16)


def _rwkv_post(ys, r, v, kd, g, p, geo, rows_out):
    nt, d = r.shape
    tm = geo["tm"]
    tn = _largest_divisor(d, (512, 256, 128))
    tok = pl.BlockSpec((tm, tn), lambda i, j: (i, j))
    tok2 = pl.BlockSpec((2, tm, tn), lambda i, j: (0, i, j))
    return pl.pallas_call(
        _rwkv_post_kernel,
        grid=(rows_out // tm, d // tn),
        in_specs=[tok, tok, tok, tok, tok2, tok,
                  pl.BlockSpec((8, tn), lambda i, j: (0, j)),
                  pl.BlockSpec((tn, tn), lambda i, j: (0, 0))],
        out_specs=tok,
        out_shape=jax.ShapeDtypeStruct((rows_out, d), BF16),
        compiler_params=_cparams(("parallel", "parallel")),
        name="rwkv_post",
    )(ys[0], ys[1], r, v, kd, g, p["post_vecs"], p["head_ones"])


def _pad_rows(a, rows):
    return jnp.concatenate([a, jnp.zeros((rows - a.shape[0],) + a.shape[1:], a.dtype)], axis=0)


def _deinterleave_cols(w, n_heads):
    d = w.shape[0]
    return w.reshape(d, n_heads, ATTN_HEAD_DIM // 2, 2).swapaxes(2, 3).reshape(d, n_heads * ATTN_HEAD_DIM)


def _rope_tables(n_tokens, tm):
    rows = n_tokens // GRID_W
    row = jnp.repeat(jnp.arange(rows, dtype=F32), GRID_W)
    col = jnp.tile(jnp.arange(GRID_W, dtype=F32), rows)
    n_freq = ATTN_HEAD_DIM // 4
    inv = ROPE_THETA ** (-jnp.arange(n_freq, dtype=F32) / n_freq)
    ang = jnp.concatenate([row[:, None] * inv, col[:, None] * inv], axis=-1)
    cos, sin = jnp.cos(ang), jnp.sin(ang)
    cos_tab = jnp.concatenate([cos, cos], axis=-1)
    sin_tab = jnp.concatenate([-sin, sin], axis=-1)
    cos_tab = jnp.concatenate([cos_tab, jnp.ones((tm, ATTN_HEAD_DIM), F32)], axis=0)
    sin_tab = jnp.concatenate([sin_tab, jnp.zeros((tm, ATTN_HEAD_DIM), F32)], axis=0)
    return cos_tab, sin_tab


def kernel(x, c, ctx, c_ctx, ada_w, ada_b, norm_pre, norm_post, ffn_w_in, ffn_w_out, attn_w_qkv, attn_w_o, attn_q_gain, attn_k_gain, rwkv_mu, rwkv_w_rkv, rwkv_w_o, rwkv_g1, rwkv_g2, rwkv_dec_w0, rwkv_dec_w1, rwkv_dec_w2, rwkv_icl_a0, rwkv_icl_a1, rwkv_icl_a2, rwkv_k_k, rwkv_k_a, rwkv_r_k, rwkv_lnx_w, rwkv_lnx_b):
    b, s_len, d = x.shape
    l_len = ctx.shape[1]
    depth = ada_w.shape[0]
    n_lat_rows, n_ctx_rows = b * s_len, b * l_len
    nt = n_lat_rows + n_ctx_rows
    tm = _largest_divisor(math.gcd(s_len, n_ctx_rows), (512, 256, 128, 64, 32, 16, 8))
    tm_prep = tm
    tm_ffn = tm
    n_lat, tps = n_lat_rows // tm, s_len // tm

    def grp_for(tile):
        return lambda i: jnp.where(i < n_lat_rows // tile, 1 + i // (s_len // tile), 0)

    geo = dict(B=b, S=s_len, L=l_len, tm=tm, tm_prep=tm_prep, tm_ffn=tm_ffn, n_lat=n_lat, tps=tps,
               n_lat_rows=n_lat_rows, grp=grp_for(tm), grp_for=grp_for)
    assert s_len % SCAN_CHUNK == 0 and l_len % SCAN_CHUNK == 0 and n_lat_rows % l_len == 0
    n_q_heads = d // ATTN_HEAD_DIM
    n_kv_heads = n_q_heads // ATTN_GROUP
    q_cols, kv_cols = n_q_heads * ATTN_HEAD_DIM, n_kv_heads * ATTN_HEAD_DIM
    tq = min(256, l_len)
    tq_lat = _largest_divisor(s_len, (512, 256, 128, 64))

    stream = jnp.concatenate([x.reshape(n_lat_rows, d), ctx.reshape(n_ctx_rows, d)], axis=0)
    mod_rows = 8 * (-(-(1 + b) // 8))
    s_all = _pad_rows(jnp.concatenate([c_ctx[None, :], c], axis=0), mod_rows)
    mods = _ada_mods(s_all, ada_w, ada_b).reshape(depth, mod_rows, 9, d)

    ffn_w_in_b = ffn_w_in.astype(BF16)
    ffn_w_out_b = ffn_w_out.astype(BF16)
    cos_tab, sin_tab = _rope_tables(s_len, tm)
    tn_heads = _largest_divisor(d, (512, 256, 128))
    idx = jnp.arange(tn_heads) // RWKV_HEAD_DIM
    head_ones = (idx[:, None] == idx[None, :]).astype(BF16)

    for i in range(depth):
        last = i == depth - 1
        j = i // 2
        mod = lambda sub: mods[i, :, 3 * sub:3 * sub + 3]
        rows_after = n_lat_rows if last else nt
        stream = _ffn_half(stream, mod(0), norm_pre[i, 0], norm_post[i, 0], ffn_w_in_b, ffn_w_out_b, i, 0, geo, nt)
        if i % 2 == 0:
            w = attn_w_qkv[j]
            w_perm = jnp.concatenate([
                _deinterleave_cols(w[:, :q_cols], n_q_heads),
                _deinterleave_cols(w[:, q_cols:q_cols + kv_cols], n_kv_heads),
                w[:, q_cols + kv_cols:]], axis=1).astype(BF16)
            deint = lambda g: g.reshape(ATTN_HEAD_DIM // 2, 2).T.reshape(ATTN_HEAD_DIM)
            gains = jnp.stack([deint(attn_q_gain[j]), deint(attn_k_gain[j])])
            q, kt, v = _qkv_project(stream, mod(1), norm_pre[i, 1], w_perm, gains, cos_tab, sin_tab, geo,
                                    n_q_heads, n_kv_heads)
            o = _attention(q, kt, v, None, batch=b, n_kv_heads=n_kv_heads, q_row0=0, n_q=s_len,
                           srcs=[(n_lat_rows, l_len), (0, s_len)], tq=tq_lat)
            if not last:
                o = _attention(q, kt, v, o, batch=b, n_kv_heads=n_kv_heads, q_row0=n_lat_rows, n_q=l_len,
                               srcs=[(n_lat_rows, l_len)], tq=tq)
            w_o = attn_w_o[j].astype(BF16)
        else:
            p = dict(
                mu8=_pad_rows(rwkv_mu[j], 8),
                w_rkv=rwkv_w_rkv[j].astype(BF16),
                dec_w1=rwkv_dec_w1[j].astype(BF16), dec_w2=rwkv_dec_w2[j].astype(BF16),
                icl_a1=rwkv_icl_a1[j].astype(BF16), icl_a2=rwkv_icl_a2[j].astype(BF16),
                g1=rwkv_g1[j].astype(BF16), g2=rwkv_g2[j].astype(BF16),
                vecs=_pad_rows(jnp.concatenate([rwkv_dec_w0[j], rwkv_icl_a0[j], rwkv_k_k[j][None], rwkv_k_a[j][None]],
                                               axis=0), 8),
                post_vecs=_pad_rows(jnp.stack([rwkv_r_k[j].reshape(d), rwkv_lnx_w[j], rwkv_lnx_b[j]]), 8),
                head_ones=head_ones,
            )
            r, v, kk, g, lw, kd, bb = _rwkv_prep(stream, mod(1), norm_pre[i, 1], p, geo)
            ys = _rwkv_scan(r, v, kk, lw, kd, bb, geo)
            o = _rwkv_post(ys, r, v, kd, g, p, geo, rows_after)
            w_o = rwkv_w_o[j].astype(BF16)
        stream = _oproj_residual(o, stream, mod(1), norm_post[i, 1], w_o, geo, rows_after)
        stream = _ffn_half(stream, mod(2), norm_pre[i, 2], norm_post[i, 2], ffn_w_in_b, ffn_w_out_b, i, 1, geo,
                           rows_after)
    return stream.reshape(b, s_len, d)
```

```python
import functools
import math

import jax
import jax.numpy as jnp
from jax import lax
from jax.experimental import pallas as pl
from jax.experimental.pallas import tpu as pltpu

F32 = jnp.float32
BF16 = jnp.bfloat16

NORM_EPS = 1e-6
LNX_EPS = 64e-5
L2N_EPS = 1e-12
ATTN_HEAD_DIM = 128
ATTN_GROUP = 4
ROPE_THETA = 10000.0
GRID_W = 64
RWKV_HEAD_DIM = 64
LANES = 128
SCAN_CHUNK = 64
SCAN_LANES = 2048
VMEM_LIMIT = 56 * 1024 * 1024
DECAY_SCALE = math.exp(-0.5)


def _cparams(sem):
    return pltpu.CompilerParams(dimension_semantics=sem, vmem_limit_bytes=VMEM_LIMIT)


def _largest_divisor(n, candidates):
    for c in candidates:
        if n % c == 0:
            return c
    raise ValueError(f"no tile in {candidates} divides {n}")


def _dot(a, b):
    return jnp.dot(a, b, preferred_element_type=F32)


def _dot_nt(a, b):
    return lax.dot_general(a, b, (((1,), (1,)), ((), ())), preferred_element_type=F32)


def _split2(a):
    hi = a.astype(BF16)
    lo = (a - hi.astype(F32)).astype(BF16)
    return hi, lo


def _mm3(a, b, nt=False):
    d = _dot_nt if nt else _dot
    return d(a[0], b[0]) + (d(a[0], b[1]) + d(a[1], b[0]))


def _rms_rows(x, eps):
    return x * lax.rsqrt(jnp.mean(x * x, axis=-1, keepdims=True) + eps)


def _modulate(x, g_pre, shift, scale):
    return _rms_rows(x, NORM_EPS) * g_pre * (1.0 + scale) + shift


def _ada_kernel(s_ref, w_ref, b_ref, o_ref):
    s = s_ref[...]
    s = s * jax.nn.sigmoid(s)
    o_ref[...] = _mm3(_split2(s), _split2(w_ref[...])) + b_ref[...]


def _ada_mods(s_all, ada_w, ada_b):
    depth, d, n = ada_w.shape
    tn = _largest_divisor(n, (1024, 512, 256, 128))
    rows = s_all.shape[0]
    return pl.pallas_call(
        _ada_kernel,
        grid=(depth, n // tn),
        in_specs=[
            pl.BlockSpec((rows, d), lambda i, j: (0, 0)),
            pl.BlockSpec((None, d, tn), lambda i, j: (i, 0, j)),
            pl.BlockSpec((None, 1, tn), lambda i, j: (i, 0, j)),
        ],
        out_specs=pl.BlockSpec((None, rows, tn), lambda i, j: (i, 0, j)),
        out_shape=jax.ShapeDtypeStruct((depth, rows, n), F32),
        compiler_params=_cparams(("arbitrary", "arbitrary")),
        name="ada_mods",
    )(s_all, ada_w, ada_b.reshape(depth, 1, n))


def _ffn_kernel(x_ref, mod_ref, gpre_ref, gpost_ref, wg_ref, wu_ref, wo_ref, o_ref, y_scr, *, n_f):
    f = pl.program_id(1)

    def step(first, last):
        if first:
            y = _modulate(x_ref[...], gpre_ref[...], mod_ref[0:1, :], mod_ref[1:2, :]).astype(BF16)
            y_scr[...] = y
        else:
            y = y_scr[...]
        g = _dot(y, wg_ref[...])
        u = _dot(y, wu_ref[...])
        a = (g * jax.nn.sigmoid(g)) * u
        acc = _dot(a.astype(BF16), wo_ref[...])
        if not first:
            acc = o_ref[...] + acc
        if last:
            h = _rms_rows(acc, NORM_EPS) * gpost_ref[...]
            acc = x_ref[...] + 0.5 * (mod_ref[2:3, :] * h)
        o_ref[...] = acc

    if n_f == 1:
        step(True, True)
    else:
        pl.when(f == 0)(lambda: step(True, False))
        pl.when(f == n_f - 1)(lambda: step(False, True))
        if n_f > 2:
            pl.when(jnp.logical_and(f > 0, f < n_f - 1))(lambda: step(False, False))


def _ffn_half(stream, mod, g_pre, g_post, w_in, w_out, layer, half, geo, rows_out):
    nt, d = stream.shape
    f_dim = w_out.shape[2]
    tm = geo["tm_ffn"]
    tf = _largest_divisor(f_dim, (512, 256, 128))
    nf = f_dim // tf
    grp = geo["grp_for"](tm)
    return pl.pallas_call(
        functools.partial(_ffn_kernel, n_f=nf),
        grid=(rows_out // tm, nf),
        in_specs=[
            pl.BlockSpec((tm, d), lambda i, f: (i, 0)),
            pl.BlockSpec((None, 3, d), lambda i, f: (grp(i), 0, 0)),
            pl.BlockSpec((1, d), lambda i, f: (0, 0)),
            pl.BlockSpec((1, d), lambda i, f: (0, 0)),
            pl.BlockSpec((None, None, d, tf), lambda i, f: (layer, half, 0, f)),
            pl.BlockSpec((None, None, d, tf), lambda i, f: (layer, half, 0, nf + f)),
            pl.BlockSpec((None, None, tf, d), lambda i, f: (layer, half, f, 0)),
        ],
        out_specs=pl.BlockSpec((tm, d), lambda i, f: (i, 0)),
        out_shape=jax.ShapeDtypeStruct((rows_out, d), F32),
        scratch_shapes=[pltpu.VMEM((tm, d), BF16)],
        compiler_params=_cparams(("parallel", "arbitrary")),
        name="ffn_half",
    )(stream, mod, g_pre.reshape(1, d), g_post.reshape(1, d), w_in, w_in, w_out)


def _oproj_kernel(o_ref, x_ref, mod_ref, gpost_ref, w_ref, out_ref):
    h = _dot(o_ref[...], w_ref[...])
    h = _rms_rows(h, NORM_EPS) * gpost_ref[...]
    out_ref[...] = x_ref[...] + mod_ref[2:3, :] * h


def _oproj_residual(o, stream, mod, g_post, w_o, geo, rows_out):
    d = stream.shape[1]
    tm = geo["tm"]
    grp = geo["grp"]
    return pl.pallas_call(
        _oproj_kernel,
        grid=(rows_out // tm,),
        in_specs=[
            pl.BlockSpec((tm, d), lambda i: (i, 0)),
            pl.BlockSpec((tm, d), lambda i: (i, 0)),
            pl.BlockSpec((None, 3, d), lambda i: (grp(i), 0, 0)),
            pl.BlockSpec((1, d), lambda i: (0, 0)),
            pl.BlockSpec((d, d), lambda i: (0, 0)),
        ],
        out_specs=pl.BlockSpec((tm, d), lambda i: (i, 0)),
        out_shape=jax.ShapeDtypeStruct((rows_out, d), F32),
        compiler_params=_cparams(("parallel",)),
        name="oproj_residual",
    )(o, stream, mod, g_post.reshape(1, d), w_o)


def _qkv_kernel(x_ref, mod_ref, gpre_ref, w_ref, gains_ref, cos_ref, sin_ref, q_ref, kt_ref, v_ref,
                *, n_q_heads, n_kv_heads, q_scale):
    hd = ATTN_HEAD_DIM
    y = _modulate(x_ref[...], gpre_ref[...], mod_ref[0:1, :], mod_ref[1:2, :]).astype(BF16)
    cos = cos_ref[...]
    sin = sin_ref[...]

    def head_post(h, gain):
        hn = _rms_rows(h, NORM_EPS) * gain
        return hn * cos + pltpu.roll(hn, hd // 2, 1) * sin

    for c in range(n_q_heads // ATTN_GROUP):
        w = ATTN_GROUP * hd
        acc = _dot(y, w_ref[:, c * w:(c + 1) * w])
        for g in range(ATTN_GROUP):
            h = head_post(acc[:, g * hd:(g + 1) * hd], gains_ref[0:1, :]) * q_scale
            q_ref[:, (c * ATTN_GROUP + g) * hd:(c * ATTN_GROUP + g + 1) * hd] = h.astype(BF16)
    q_cols = n_q_heads * hd
    kv_cols = n_kv_heads * hd
    acc = _dot(y, w_ref[:, q_cols:q_cols + kv_cols])
    for h_i in range(n_kv_heads):
        kh = head_post(acc[:, h_i * hd:(h_i + 1) * hd], gains_ref[1:2, :])
        kt_ref[h_i * hd:(h_i + 1) * hd, :] = kh.T.astype(BF16)
    v_ref[...] = _dot(y, w_ref[:, q_cols + kv_cols:]).astype(BF16)


def _qkv_project(stream, mod, g_pre, w_qkv, gains, cos_tab, sin_tab, geo, n_q_heads, n_kv_heads):
    nt, d = stream.shape
    tm = geo["tm"]
    grp = geo["grp"]
    n_lat, tps = geo["n_lat"], geo["tps"]
    hd = ATTN_HEAD_DIM
    q_cols, kv_cols = n_q_heads * hd, n_kv_heads * hd
    rope_blk = lambda i: jnp.where(i < n_lat, i % tps, tps)
    return pl.pallas_call(
        functools.partial(_qkv_kernel, n_q_heads=n_q_heads, n_kv_heads=n_kv_heads,
                          q_scale=hd ** -0.5 * math.log2(math.e)),
        grid=(nt // tm,),
        in_specs=[
            pl.BlockSpec((tm, d), lambda i: (i, 0)),
            pl.BlockSpec((None, 3, d), lambda i: (grp(i), 0, 0)),
            pl.BlockSpec((1, d), lambda i: (0, 0)),
            pl.BlockSpec((d, q_cols + 2 * kv_cols), lambda i: (0, 0)),
            pl.BlockSpec((2, hd), lambda i: (0, 0)),
            pl.BlockSpec((tm, hd), lambda i: (rope_blk(i), 0)),
            pl.BlockSpec((tm, hd), lambda i: (rope_blk(i), 0)),
        ],
        out_specs=[
            pl.BlockSpec((tm, q_cols), lambda i: (i, 0)),
            pl.BlockSpec((kv_cols, tm), lambda i: (0, i)),
            pl.BlockSpec((tm, kv_cols), lambda i: (i, 0)),
        ],
        out_shape=[
            jax.ShapeDtypeStruct((nt, q_cols), BF16),
            jax.ShapeDtypeStruct((kv_cols, nt), BF16),
            jax.ShapeDtypeStruct((nt, kv_cols), BF16),
        ],
        compiler_params=_cparams(("parallel",)),
        name="qkv_project",
    )(stream, mod, g_pre.reshape(1, d), w_qkv, gains, cos_tab, sin_tab)


def _attn_kernel(*refs, src_lens, tk):
    q_ref = refs[0]
    o_ref = refs[-1]
    hd = ATTN_HEAD_DIM
    tq = q_ref.shape[0]
    q = q_ref[...]
    qs = jnp.concatenate([q[:, g * hd:(g + 1) * hd] for g in range(ATTN_GROUP)], axis=0)
    rows = ATTN_GROUP * tq
    m = jnp.full((rows, 1), -jnp.inf, F32)
    acc = jnp.zeros((rows, 2 * hd), F32)
    for s_i, n_keys in enumerate(src_lens):
        kt_ref, v_ref = refs[1 + 2 * s_i], refs[2 + 2 * s_i]
        step = min(tk, n_keys)
        ones = jnp.ones((step, hd), BF16)
        for c in range(n_keys // step):
            s = _dot(qs, kt_ref[:, c * step:(c + 1) * step])
            m_new = jnp.maximum(m, jnp.max(s, axis=-1, keepdims=True))
            p = jnp.exp2(s - m_new).astype(BF16)
            v_aug = jnp.concatenate([v_ref[c * step:(c + 1) * step, :], ones], axis=1)
            acc = jnp.exp2(m - m_new) * acc + _dot(p, v_aug)
            m = m_new
    o = acc[:, :hd] * (1.0 / acc[:, hd:])
    o_ref[...] = jnp.concatenate([o[g * tq:(g + 1) * tq] for g in range(ATTN_GROUP)], axis=1).astype(BF16)


def _attention(q, kt, v, o_prev, *, batch, n_kv_heads, q_row0, n_q, srcs, tq):
    hd = ATTN_HEAD_DIM
    gw = ATTN_GROUP * hd
    nt = q.shape[0]
    q_blk0 = q_row0 // tq
    in_specs = [pl.BlockSpec((tq, gw), lambda b, h, i: (q_blk0 + b * (n_q // tq) + i, h))]
    args = [q]
    for row0, n_keys in srcs:
        blk0 = row0 // n_keys
        in_specs.append(pl.BlockSpec((hd, n_keys), lambda b, h, i, blk0=blk0: (h, blk0 + b)))
        in_specs.append(pl.BlockSpec((n_keys, hd), lambda b, h, i, blk0=blk0: (blk0 + b, h)))
        args += [kt, v]
    aliases = {}
    if o_prev is not None:
        in_specs.append(pl.BlockSpec(memory_space=pl.ANY))
        args.append(o_prev)
        aliases = {len(args) - 1: 0}
    kern = functools.partial(_attn_kernel, src_lens=tuple(n for _, n in srcs), tk=256)
    if o_prev is not None:
        inner = kern
        kern = lambda *refs: inner(*refs[:-2], refs[-1])
    return pl.pallas_call(
        kern,
        grid=(batch, n_kv_heads, n_q // tq),
        in_specs=in_specs,
        out_specs=pl.BlockSpec((tq, gw), lambda b, h, i: (q_blk0 + b * (n_q // tq) + i, h)),
        out_shape=jax.ShapeDtypeStruct((nt, n_kv_heads * gw), BF16),
        input_output_aliases=aliases,
        compiler_params=_cparams(("parallel", "parallel", "arbitrary")),
        name="attention",
    )(*args)


def _head_sums(x, e_ref):
    hi, lo = _split2(x)
    e = e_ref[...]
    return _dot(hi, e) + _dot(lo, e)


def _rwkv_prep_kernel(x_ref, xp_ref, xn_ref, mod_ref, gpre_ref, mu_ref, w_ref, dw1_ref, ia1_ref, g1_ref, lora2_ref,
                      g2_ref, vec_ref, e_ref,
                      rvkg_ref, lw_ref, kb_ref,
                      xr_scr, xk_scr, xv_scr, hw_scr, ha_scr, hg_scr, *, n_lat_rows, seq_lat, seq_ctx):
    i = pl.program_id(0)
    j = pl.program_id(1)
    tm = x_ref.shape[0]

    def mixes():
        modulate = lambda x: _modulate(x, gpre_ref[...], mod_ref[0:1, :], mod_ref[1:2, :])
        y = modulate(x_ref[...])
        row = i * tm + lax.broadcasted_iota(jnp.int32, (tm, 1), 0)
        is_lat = row < n_lat_rows
        pos = jnp.where(is_lat, row % seq_lat, (row - n_lat_rows) % seq_ctx)
        seq_len = jnp.where(is_lat, seq_lat, seq_ctx)
        loc = lax.broadcasted_iota(jnp.int32, (tm, 1), 0)
        up = jnp.where(loc == 0, modulate(xp_ref[7:8, :]), pltpu.roll(y, 1, 0))
        up = jnp.where(pos == 0, 0.0, up)
        dn = jnp.where(loc == tm - 1, modulate(xn_ref[0:1, :]), pltpu.roll(y, tm - 1, 0))
        dn = jnp.where(pos == seq_len - 1, 0.0, dn)
        xx = 0.5 * (up + dn) - y
        mix = lambda m: (y + xx * mu_ref[m:m + 1, :]).astype(BF16)
        xr, xw, xk, xv, xa, xg = (mix(m) for m in range(6))
        hw = [jnp.tanh(_dot(xw, dw1_ref[dr])).astype(BF16) for dr in range(2)]
        ha = [_dot(xa, ia1_ref[dr]).astype(BF16) for dr in range(2)]
        hg = jax.nn.sigmoid(_dot(xg, g1_ref[...])).astype(BF16)
        xr_scr[...], xk_scr[...], xv_scr[...], hg_scr[...] = xr, xk, xv, hg
        for dr in range(2):
            hw_scr[dr], ha_scr[dr] = hw[dr], ha[dr]
        return xr, xk, xv, hw, ha, hg

    def column_tile(first):
        if first:
            xr, xk, xv, hw, ha, hg = mixes()
        else:
            xr, xk, xv, hg = xr_scr[...], xk_scr[...], xv_scr[...], hg_scr[...]
            hw, ha = [hw_scr[dr] for dr in range(2)], [ha_scr[dr] for dr in range(2)]
        rvkg_ref[0] = _dot(xr, w_ref[0]).astype(BF16)
        k = _dot(xk, w_ref[1])
        rvkg_ref[1] = _dot(xv, w_ref[2]).astype(BF16)
        rvkg_ref[3] = _dot(hg, g2_ref[...]).astype(BF16)
        kx = k * vec_ref[4:5, :]
        kk = kx * lax.rsqrt(_head_sums(kx * kx, e_ref) + L2N_EPS)
        rvkg_ref[2] = kk.astype(BF16)
        k_a = vec_ref[5:6, :]
        for dr in range(2):
            w_pre = vec_ref[dr:dr + 1, :] + _dot(hw[dr], lora2_ref[dr])
            lw_ref[dr] = -DECAY_SCALE * jax.nn.sigmoid(w_pre)
            a = jax.nn.sigmoid(vec_ref[2 + dr:3 + dr, :] + _dot(ha[dr], lora2_ref[2 + dr]))
            kb_ref[dr] = (k * (1.0 + (a - 1.0) * k_a)).astype(BF16)
            kb_ref[2 + dr] = (kk * a).astype(BF16)

    pl.when(j == 0)(lambda: column_tile(True))
    pl.when(j != 0)(lambda: column_tile(False))


def _rwkv_prep(stream, mod, g_pre, p, geo):
    nt, d = stream.shape
    tm = geo["tm_prep"]
    grp = geo["grp_for"](tm)
    tn = _largest_divisor(d, (256, 128))
    w_tiles = p["w_rkv"].reshape(3, d, d // tn, tn).transpose(2, 0, 1, 3)
    lo_w = p["dec_w1"].shape[-1]
    lo_a = p["icl_a1"].shape[-1]
    lo_g = p["g1"].shape[-1]
    hb = tm // 8
    n_blk8 = nt // 8
    const3 = lambda i, j: (0, 0, 0)
    col3 = lambda i, j: (0, 0, j)
    tok2 = pl.BlockSpec((2, tm, tn), lambda i, j: (0, i, j))
    tok4 = pl.BlockSpec((4, tm, tn), lambda i, j: (0, i, j))
    kern = functools.partial(_rwkv_prep_kernel, n_lat_rows=geo["n_lat_rows"], seq_lat=geo["S"], seq_ctx=geo["L"])
    return pl.pallas_call(
        kern,
        grid=(nt // tm, d // tn),
        in_specs=[
            pl.BlockSpec((tm, d), lambda i, j: (i, 0)),
            pl.BlockSpec((8, d), lambda i, j: (jnp.maximum(i * hb - 1, 0), 0)),
            pl.BlockSpec((8, d), lambda i, j: (jnp.minimum((i + 1) * hb, n_blk8 - 1), 0)),
            pl.BlockSpec((None, 3, d), lambda i, j: (grp(i), 0, 0)),
            pl.BlockSpec((1, d), lambda i, j: (0, 0)),
            pl.BlockSpec((8, d), lambda i, j: (0, 0)),
            pl.BlockSpec((None, 3, d, tn), lambda i, j: (j, 0, 0, 0)),
            pl.BlockSpec((2, d, lo_w), const3),
            pl.BlockSpec((2, d, lo_a), const3),
            pl.BlockSpec((d, lo_g), lambda i, j: (0, 0)),
            pl.BlockSpec((4, lo_w, tn), col3),
            pl.BlockSpec((lo_g, tn), lambda i, j: (0, j)),
            pl.BlockSpec((8, tn), lambda i, j: (0, j)),
            pl.BlockSpec((tn, tn), lambda i, j: (0, 0)),
        ],
        out_specs=[tok4, tok2, tok4],
        out_shape=[jax.ShapeDtypeStruct((4, nt, d), BF16), jax.ShapeDtypeStruct((2, nt, d), F32),
                   jax.ShapeDtypeStruct((4, nt, d), BF16)],
        scratch_shapes=[pltpu.VMEM((tm, d), BF16)] * 3 + [
            pltpu.VMEM((2, tm, lo_w), BF16), pltpu.VMEM((2, tm, lo_a), BF16), pltpu.VMEM((tm, lo_g), BF16)],
        compiler_params=_cparams(("parallel", "arbitrary")),
        name="rwkv_prep",
    )(stream, stream, stream, mod, g_pre.reshape(1, d), p["mu8"], w_tiles, p["dec_w1"], p["icl_a1"], p["g1"],
      p["lora2"], p["g2"], p["vecs"], p["head_ones"])


def _scan_chunk(r, v, kk, lw, kd, bb, state, order, masks):
    bd_state, eye, m0, m1, sib = masks
    tri, bd_strict, bd_incl = zip(*order)
    c = r[0].shape[0]
    c2 = 2 * c
    each = lambda fn, *cols: [fn(*args) for args in zip(*cols)]
    unstack = lambda a: a[:c] * m0 + a[c:] * m1

    def split3(x):
        hi = x.astype(BF16)
        res = x - hi.astype(F32)
        mid = res.astype(BF16)
        return jnp.concatenate([hi, mid, (res - mid.astype(F32)).astype(BF16)], axis=1)

    cum3 = each(lambda x, tr: _dot(tr, split3(x)), lw, tri)
    cum = each(lambda x: x[:, :LANES] + (x[:, LANES:2 * LANES] + x[:, 2 * LANES:]), cum3)
    tot = each(lambda x: jnp.sum(x, axis=0, keepdims=True), lw)
    e_neg = each(lambda x: jnp.exp(-x), cum)
    rt = each(lambda x, cm: x * jnp.exp(cm), r, cum)
    kkt = each(lambda x, cm, w: x * jnp.exp(cm - w), kk, cum, lw)
    lhs = each(lambda k_, r_: jnp.concatenate([(k_ * m0).astype(BF16), (k_ * m1).astype(BF16),
                                               (r_ * m0).astype(BF16), (r_ * m1).astype(BF16)], axis=0), kkt, rt)

    def make_rhs(kd_, bb_, en):
        kh = (kd_ * en).astype(BF16)
        bh = (bb_ * en).astype(BF16)
        return jnp.concatenate([kh, kh, bh, bh], axis=0)

    rhs = each(make_rhs, kd, bb, e_neg)
    a_all = each(_dot_nt, lhs, rhs)
    a_kkb = each(lambda a, strict: jnp.where(strict, a[:c2, c2:], 0.0), a_all, bd_strict)
    t_inv = each(lambda a: eye - jnp.where(sib[0], a, 0.0), a_kkb)
    for lvl in range(1, len(sib)):
        def merge(t, a, lvl=lvl):
            tb = t.astype(BF16)
            return t - _dot(_dot(tb, jnp.where(sib[lvl], a, 0.0).astype(BF16)).astype(BF16), tb)
        t_inv = each(merge, t_inv, a_kkb)
    st = each(lambda s: s.astype(BF16), state)
    v2 = each(lambda x: jnp.concatenate([x.astype(BF16)] * 2, axis=0), v)
    x2 = each(lambda l_, s_, a, v_, strict: _dot_nt(l_[:c2], s_) + _dot(jnp.where(strict, a[:c2, :c2], 0.0).astype(BF16), v_),
              lhs, st, a_all, v2, bd_strict)
    u2 = each(lambda t, x: _dot(t.astype(BF16), x.astype(BF16)), t_inv, x2)

    def outputs(l_, s_, a, v_, u_, incl):
        a_rk = jnp.where(incl, a[c2:, :c2], 0.0).astype(BF16)
        a_rb_neg = jnp.where(incl, -a[c2:, c2:], 0.0).astype(BF16)
        y2 = _dot_nt(l_[c2:], s_) + _dot(jnp.concatenate([a_rk, a_rb_neg], axis=1),
                                         jnp.concatenate([v_, u_.astype(BF16)], axis=0))
        return unstack(y2)

    y = each(outputs, lhs, st, a_all, v2, u2, bd_incl)

    def state_update(s_, v_, u_, kd_, bb_, cm, tt):
        e_rem = jnp.exp(tt - cm)
        vu_t = jnp.concatenate([v_.astype(F32), -unstack(u_)], axis=0).T.astype(BF16)
        ds = _dot(vu_t, jnp.concatenate([(kd_ * e_rem).astype(BF16), (bb_ * e_rem).astype(BF16)], axis=0))
        return s_ * jnp.exp(tt) + jnp.where(bd_state, ds, 0.0)

    new_state = each(state_update, state, v, u2, kd, bb, cum, tot)
    return y, new_state


def _scan_kernel(*refs):
    fwd, bwd = refs[0:6], refs[6:12]
    y_refs = refs[12:14]
    state_scr = refs[14]
    t = pl.program_id(2)
    c = fwd[0].shape[0]
    n_grp = fwd[0].shape[1] // LANES
    hd = RWKV_HEAD_DIM

    @pl.when(t == 0)
    def _():
        state_scr[...] = jnp.zeros_like(state_scr)

    ri = lax.broadcasted_iota(jnp.int32, (2 * c, 2 * c), 0)
    ci = lax.broadcasted_iota(jnp.int32, (2 * c, 2 * c), 1)
    rc = lax.broadcasted_iota(jnp.int32, (c, c), 0)
    cc = lax.broadcasted_iota(jnp.int32, (c, c), 1)
    same_head = (ri // c) == (ci // c)

    def order_masks(sgn):
        order = jnp.where(same_head, (ri - ci) * sgn, -1)
        tri = jnp.where((rc - cc) * sgn >= 0, 1.0, 0.0).astype(BF16)
        return tri, order > 0, order >= 0

    eye = jnp.where(ri == ci, 1.0, 0.0).astype(F32)
    sr = lax.broadcasted_iota(jnp.int32, (LANES, LANES), 0)
    sc = lax.broadcasted_iota(jnp.int32, (LANES, LANES), 1)
    bd_state = (sr // hd) == (sc // hd)
    lane = lax.broadcasted_iota(jnp.int32, (1, LANES), 1)
    m0 = jnp.where(lane < hd, 1.0, 0.0).astype(F32)
    m1 = 1.0 - m0
    blk = lambda k: (ri >> k) == (ci >> k)
    sib = [blk(1)] + [blk(k + 1) & jnp.logical_not(blk(k)) for k in range(1, int(math.log2(c)))]
    lanes = [slice(gi * LANES, (gi + 1) * LANES) for gi in range(n_grp)]
    cols = lambda i: [ref[:, sl] for ref in (fwd[i], bwd[i]) for sl in lanes]
    order = [order_masks(1)] * n_grp + [order_masks(-1)] * n_grp
    y, new_state = _scan_chunk(*[cols(i) for i in range(6)], [state_scr[gi] for gi in range(2 * n_grp)], order,
                               (bd_state, eye, m0, m1, sib))
    for gi in range(2 * n_grp):
        y_refs[gi // n_grp][:, lanes[gi % n_grp]] = y[gi].astype(BF16)
        state_scr[gi] = new_state[gi]


def _rwkv_scan(rvkg, lw, kb, geo):
    nt, d = lw.shape[1:]
    c = SCAN_CHUNK
    b, s_len, l_len = geo["B"], geo["S"], geo["L"]
    n_lc, n_sc = l_len // c, s_len // c
    wl = SCAN_LANES if d % SCAN_LANES == 0 else LANES
    ctx_blk0 = (b * s_len) // c

    def row_blk(bi, t, reverse):
        t_ctx = n_lc - 1 - t if reverse else t
        t_lat = n_sc - 1 - (t - n_lc) if reverse else t - n_lc
        return jnp.where(t < n_lc, ctx_blk0 + bi * n_lc + t_ctx, bi * n_sc + t_lat)

    def specs(reverse):
        plane = lambda k: pl.BlockSpec((None, c, wl), lambda bi, h, t: (k, row_blk(bi, t, reverse), h))
        return [plane(0), plane(1), plane(2), plane(int(reverse)), plane(int(reverse)), plane(2 + int(reverse))]

    return pl.pallas_call(
        _scan_kernel,
        grid=(b, d // wl, n_lc + n_sc),
        in_specs=specs(False) + specs(True),
        out_specs=[pl.BlockSpec((c, wl), lambda bi, h, t, rev=rev: (row_blk(bi, t, rev), h)) for rev in (False, True)],
        out_shape=[jax.ShapeDtypeStruct((nt, d), BF16)] * 2,
        scratch_shapes=[pltpu.VMEM((2 * (wl // LANES), LANES, LANES), F32)],
        compiler_params=_cparams(("parallel", "parallel", "arbitrary")),
        name="rwkv_scan",
    )(*([rvkg, rvkg, rvkg, lw, kb, kb] * 2))


def _rwkv_post_kernel(yf_ref, yb_ref, r_ref, v_ref, kd_ref, g_ref, vec_ref, e_ref, o_ref):
    inv_n = 1.0 / RWKV_HEAD_DIM
    o = yf_ref[...].astype(F32) + yb_ref[...].astype(F32)
    mean = _head_sums(o, e_ref) * inv_n
    oc = o - mean
    var = _head_sums(oc * oc, e_ref) * inv_n
    o = oc * lax.rsqrt(var + LNX_EPS) * vec_ref[1:2, :] + vec_ref[2:3, :]
    k_sum = kd_ref[0].astype(F32) + kd_ref[1].astype(F32)
    bonus = _head_sums(r_ref[...].astype(F32) * k_sum * vec_ref[0:1, :], e_ref)
    o = o + bonus * v_ref[...].astype(F32)
    o_ref[...] = (o * g_ref[...].astype(F32)).astype(BF16)


def _rwkv_post(ys, rvkg, kb, p, geo, rows_out):
    nt, d = rvkg.shape[1:]
    tm = geo["tm"]
    tn = _largest_divisor(d, (512, 256, 128))
    tok = pl.BlockSpec((tm, tn), lambda i, j: (i, j))
    tok2 = pl.BlockSpec((2, tm, tn), lambda i, j: (0, i, j))
    plane = lambda k: pl.BlockSpec((None, tm, tn), lambda i, j: (k, i, j))
    return pl.pallas_call(
        _rwkv_post_kernel,
        grid=(rows_out // tm, d // tn),
        in_specs=[tok, tok, plane(0), plane(1), tok2, plane(3),
                  pl.BlockSpec((8, tn), lambda i, j: (0, j)),
                  pl.BlockSpec((tn, tn), lambda i, j: (0, 0))],
        out_specs=tok,
        out_shape=jax.ShapeDtypeStruct((rows_out, d), BF16),
        compiler_params=_cparams(("parallel", "parallel")),
        name="rwkv_post",
    )(ys[0], ys[1], rvkg, rvkg, kb, rvkg, p["post_vecs"], p["head_ones"])


def _pad_rows(a, rows):
    return jnp.concatenate([a, jnp.zeros((rows - a.shape[0],) + a.shape[1:], a.dtype)], axis=0)


def _deinterleave_cols(w, n_heads):
    d = w.shape[0]
    return w.reshape(d, n_heads, ATTN_HEAD_DIM // 2, 2).swapaxes(2, 3).reshape(d, n_heads * ATTN_HEAD_DIM)


def _rope_tables(n_tokens, tm):
    rows = n_tokens // GRID_W
    row = jnp.repeat(jnp.arange(rows, dtype=F32), GRID_W)
    col = jnp.tile(jnp.arange(GRID_W, dtype=F32), rows)
    n_freq = ATTN_HEAD_DIM // 4
    inv = ROPE_THETA ** (-jnp.arange(n_freq, dtype=F32) / n_freq)
    ang = jnp.concatenate([row[:, None] * inv, col[:, None] * inv], axis=-1)
    cos, sin = jnp.cos(ang), jnp.sin(ang)
    cos_tab = jnp.concatenate([cos, cos], axis=-1)
    sin_tab = jnp.concatenate([-sin, sin], axis=-1)
    cos_tab = jnp.concatenate([cos_tab, jnp.ones((tm, ATTN_HEAD_DIM), F32)], axis=0)
    sin_tab = jnp.concatenate([sin_tab, jnp.zeros((tm, ATTN_HEAD_DIM), F32)], axis=0)
    return cos_tab, sin_tab


def kernel(x, c, ctx, c_ctx, ada_w, ada_b, norm_pre, norm_post, ffn_w_in, ffn_w_out, attn_w_qkv, attn_w_o, attn_q_gain, attn_k_gain, rwkv_mu, rwkv_w_rkv, rwkv_w_o, rwkv_g1, rwkv_g2, rwkv_dec_w0, rwkv_dec_w1, rwkv_dec_w2, rwkv_icl_a0, rwkv_icl_a1, rwkv_icl_a2, rwkv_k_k, rwkv_k_a, rwkv_r_k, rwkv_lnx_w, rwkv_lnx_b):
    b, s_len, d = x.shape
    l_len = ctx.shape[1]
    depth = ada_w.shape[0]
    n_lat_rows, n_ctx_rows = b * s_len, b * l_len
    nt = n_lat_rows + n_ctx_rows
    tm = _largest_divisor(math.gcd(s_len, n_ctx_rows), (512, 256, 128, 64, 32, 16, 8))
    tm_prep = tm
    tm_ffn = tm
    n_lat, tps = n_lat_rows // tm, s_len // tm

    def grp_for(tile):
        return lambda i: jnp.where(i < n_lat_rows // tile, 1 + i // (s_len // tile), 0)

    geo = dict(B=b, S=s_len, L=l_len, tm=tm, tm_prep=tm_prep, tm_ffn=tm_ffn, n_lat=n_lat, tps=tps,
               n_lat_rows=n_lat_rows, grp=grp_for(tm), grp_for=grp_for)
    assert s_len % SCAN_CHUNK == 0 and l_len % SCAN_CHUNK == 0 and n_lat_rows % l_len == 0
    n_q_heads = d // ATTN_HEAD_DIM
    n_kv_heads = n_q_heads // ATTN_GROUP
    q_cols, kv_cols = n_q_heads * ATTN_HEAD_DIM, n_kv_heads * ATTN_HEAD_DIM
    tq = min(256, l_len)
    tq_lat = _largest_divisor(s_len, (512, 256, 128, 64))

    stream = jnp.concatenate([x.reshape(n_lat_rows, d), ctx.reshape(n_ctx_rows, d)], axis=0)
    mod_rows = 8 * (-(-(1 + b) // 8))
    s_all = _pad_rows(jnp.concatenate([c_ctx[None, :], c], axis=0), mod_rows)
    mods = _ada_mods(s_all, ada_w, ada_b).reshape(depth, mod_rows, 9, d)

    ffn_w_in_b = ffn_w_in.astype(BF16)
    ffn_w_out_b = ffn_w_out.astype(BF16)
    cos_tab, sin_tab = _rope_tables(s_len, tm)
    tn_heads = _largest_divisor(d, (512, 256, 128))
    idx = jnp.arange(tn_heads) // RWKV_HEAD_DIM
    head_ones = (idx[:, None] == idx[None, :]).astype(BF16)

    for i in range(depth):
        last = i == depth - 1
        j = i // 2
        mod = lambda sub: mods[i, :, 3 * sub:3 * sub + 3]
        rows_after = n_lat_rows if last else nt
        stream = _ffn_half(stream, mod(0), norm_pre[i, 0], norm_post[i, 0], ffn_w_in_b, ffn_w_out_b, i, 0, geo, nt)
        if i % 2 == 0:
            w = attn_w_qkv[j]
            w_perm = jnp.concatenate([
                _deinterleave_cols(w[:, :q_cols], n_q_heads),
                _deinterleave_cols(w[:, q_cols:q_cols + kv_cols], n_kv_heads),
                w[:, q_cols + kv_cols:]], axis=1).astype(BF16)
            deint = lambda g: g.reshape(ATTN_HEAD_DIM // 2, 2).T.reshape(ATTN_HEAD_DIM)
            gains = jnp.stack([deint(attn_q_gain[j]), deint(attn_k_gain[j])])
            q, kt, v = _qkv_project(stream, mod(1), norm_pre[i, 1], w_perm, gains, cos_tab, sin_tab, geo,
                                    n_q_heads, n_kv_heads)
            o = _attention(q, kt, v, None, batch=b, n_kv_heads=n_kv_heads, q_row0=0, n_q=s_len,
                           srcs=[(n_lat_rows, l_len), (0, s_len)], tq=tq_lat)
            if not last:
                o = _attention(q, kt, v, o, batch=b, n_kv_heads=n_kv_heads, q_row0=n_lat_rows, n_q=l_len,
                               srcs=[(n_lat_rows, l_len)], tq=tq)
            w_o = attn_w_o[j].astype(BF16)
        else:
            p = dict(
                mu8=_pad_rows(rwkv_mu[j], 8),
                w_rkv=rwkv_w_rkv[j].astype(BF16),
                dec_w1=rwkv_dec_w1[j].astype(BF16),
                icl_a1=rwkv_icl_a1[j].astype(BF16),
                lora2=jnp.concatenate([rwkv_dec_w2[j], rwkv_icl_a2[j]], axis=0).astype(BF16),
                g1=rwkv_g1[j].astype(BF16), g2=rwkv_g2[j].astype(BF16),
                vecs=_pad_rows(jnp.concatenate([rwkv_dec_w0[j], rwkv_icl_a0[j], rwkv_k_k[j][None], rwkv_k_a[j][None]],
                                               axis=0), 8),
                post_vecs=_pad_rows(jnp.stack([rwkv_r_k[j].reshape(d), rwkv_lnx_w[j], rwkv_lnx_b[j]]), 8),
                head_ones=head_ones,
            )
            rvkg, lw, kb = _rwkv_prep(stream, mod(1), norm_pre[i, 1], p, geo)
            ys = _rwkv_scan(rvkg, lw, kb, geo)
            o = _rwkv_post(ys, rvkg, kb, p, geo, rows_after)
            w_o = rwkv_w_o[j].astype(BF16)
        stream = _oproj_residual(o, stream, mod(1), norm_post[i, 1], w_o, geo, rows_after)
        stream = _ffn_half(stream, mod(2), norm_pre[i, 2], norm_post[i, 2], ffn_w_in_b, ffn_w_out_b, i, 1, geo,
                           rows_after)
    return stream.reshape(b, s_len, d)
```
